```python
import math
import jax
import jax.numpy as jnp
from jax import lax
import numpy as np

D_MODEL = 1024
BATCH = 4
SEQ = 4096
DEPTH = 4
DEC_BATCH = 128
DEC_SEQ = 4
PAST_LEN = 2048
PAGE_SIZE = 128

HEAD_DIM = 64
SB_HEADS = 8
SB_WIDTH = SB_HEADS * HEAD_DIM
DIFF_HEADS = 4
DIFF_QK = DIFF_HEADS * 2 * HEAD_DIM
DIFF_V = DIFF_HEADS * 2 * HEAD_DIM
ATTN_SPLITS = (SB_WIDTH, 2 * SB_WIDTH, 3 * SB_WIDTH, 3 * SB_WIDTH + DIFF_QK, 3 * SB_WIDTH + 2 * DIFF_QK)
ATTN_IN = 3 * SB_WIDTH + 2 * DIFF_QK + DIFF_V
ATTN_OUT = SB_WIDTH + DIFF_V
Q_BLOCK = 128
ROPE_THETA = 10000.0
NEG_INF = -1e30

SSD_HEADS = 16
SSD_HEAD_DIM = 64
SSD_WIDTH = SSD_HEADS * SSD_HEAD_DIM
SSD_GROUPS = 4
SSD_STATE = 128
SSD_CHUNK = 128
CONV_WIDTH = 4
SSD_CONV_CH = SSD_WIDTH + 2 * SSD_GROUPS * SSD_STATE
LRU_WIDTH = 1024
LRU_BLOCKS = 16
LRU_BLOCK_DIM = LRU_WIDTH // LRU_BLOCKS
LRU_C = 8.0
REC_SPLITS = (SSD_WIDTH, SSD_WIDTH + SSD_CONV_CH, SSD_WIDTH + SSD_CONV_CH + SSD_HEADS, SSD_WIDTH + SSD_CONV_CH + SSD_HEADS + LRU_WIDTH)
REC_IN = SSD_WIDTH + SSD_CONV_CH + SSD_HEADS + 2 * LRU_WIDTH
REC_OUT = SSD_WIDTH + LRU_WIDTH

FFN_HIDDEN = -(-8 * D_MODEL // (3 * 256)) * 256
N_ATTN_LAYERS = (DEPTH + 1) // 2
N_REC_LAYERS = DEPTH // 2
EPS = 1e-6

kernel_name = 'hybrid_sb_diff_ssd_rglru_adaln_decode_step'


def rmsnorm(x, g):
    xf = x.astype(jnp.float32)
    y = xf * lax.rsqrt(jnp.mean(xf * xf, axis=-1, keepdims=True) + EPS)
    return (y * g.astype(jnp.float32)).astype(x.dtype)


def modulate(x, g, shift, scale):
    return rmsnorm(x, g) * (1.0 + scale[:, None, :]) + shift[:, None, :]


def rope(x, pos):
    half = x.shape[-1] // 2
    inv = ROPE_THETA ** (-jnp.arange(half, dtype=jnp.float32) / half)
    ang = pos.astype(jnp.float32)[:, None] * inv[None, :]
    shape = (pos.shape[0],) + (1,) * (x.ndim - 3) + (half,)
    cos = jnp.cos(ang).reshape(shape)
    sin = jnp.sin(ang).reshape(shape)
    x1 = x[..., :half].astype(jnp.float32)
    x2 = x[..., half:].astype(jnp.float32)
    return jnp.concatenate([x1 * cos - x2 * sin, x2 * cos + x1 * sin], axis=-1).astype(x.dtype)


def causal_conv(x, buf, w, bias):
    l = x.shape[1]
    xp = jnp.concatenate([buf.astype(x.dtype), x], axis=1)
    y = bias
    for k in range(CONV_WIDTH):
        y = y + xp[:, k:k + l] * w[k]
    return y, xp[:, xp.shape[1] - (CONV_WIDTH - 1):]


def weighted_sum(w, vs):
    o, off = None, 0
    for v in vs:
        part = jnp.einsum('bhqs,bshe->bqhe', w[..., off:off + v.shape[1]], v)
        o = part if o is None else o + part
        off += v.shape[1]
    return o


def sb_weights(z, mask):
    log_beta = jax.nn.log_sigmoid(z)
    log_keep = jnp.where(mask, jax.nn.log_sigmoid(-z), 0.0)
    rest = lax.cumsum(log_keep, axis=3, reverse=True) - log_keep
    return jnp.where(mask, jnp.exp(log_beta + rest), 0.0)


def sb_attend(q, q_pos, ks, vs, k_pos):
    z = jnp.concatenate([jnp.einsum('bqhd,bshd->bhqs', q, k, preferred_element_type=jnp.float32) for k in ks], axis=-1)
    z = z * (HEAD_DIM ** -0.5)
    mask = k_pos[None, :] < q_pos[:, None]
    w = sb_weights(z, mask).astype(vs[0].dtype)
    return weighted_sum(w, vs)


def diff_attend(q, q_pos, ks, vs, k_pos, lam):
    z = jnp.concatenate([jnp.einsum('bqhmd,bshmd->bhmqs', q, k, preferred_element_type=jnp.float32) for k in ks], axis=-1)
    z = z * (HEAD_DIM ** -0.5)
    mask = k_pos[None, :] <= q_pos[:, None]
    pr = jax.nn.softmax(jnp.where(mask, z, NEG_INF), axis=-1)
    w = (pr[:, :, 0] - lam * pr[:, :, 1]).astype(vs[0].dtype)
    return weighted_sum(w, vs)


def over_query_blocks(fn, q, q_pos):
    b, l = q.shape[:2]
    nb = l // Q_BLOCK
    qb = jnp.moveaxis(q.reshape((b, nb, Q_BLOCK) + q.shape[2:]), 1, 0)
    pb = q_pos.reshape(nb, Q_BLOCK)
    o = lax.map(lambda a: fn(a[0], a[1]), (qb, pb))
    return jnp.moveaxis(o, 0, 1).reshape((b, l) + o.shape[3:])


def ssd_scan(x, dt, A, B, C, h0):
    b, l, H, P = x.shape
    G, N = B.shape[2], B.shape[3]
    R = H // G
    Q = SSD_CHUNK if l % SSD_CHUNK == 0 else l
    nc = l // Q
    f32 = jnp.float32
    xdt = (x.astype(f32) * dt[..., None]).reshape(b, nc, Q, G, R, P)
    Bc = B.astype(f32).reshape(b, nc, Q, G, N)
    Cc = C.astype(f32).reshape(b, nc, Q, G, N)
    cs = jnp.cumsum((dt * A).reshape(b, nc, Q, G, R), axis=2)
    causal = jnp.tril(jnp.ones((Q, Q), dtype=bool))[None, None, :, :, None, None]
    seg = cs[:, :, :, None] - cs[:, :, None, :]
    decay = jnp.where(causal, jnp.exp(jnp.where(causal, seg, 0.0)), 0.0)
    cb = jnp.einsum('bcqgn,bcsgn->bcqsg', Cc, Bc)
    y_diag = jnp.einsum('bcqsgr,bcsgrp->bcqgrp', cb[..., None] * decay, xdt)
    decay_end = jnp.exp(cs[:, :, -1:] - cs)
    chunk_states = jnp.einsum('bcsgn,bcsgrp->bcgrpn', Bc, xdt * decay_end[..., None])
    chunk_decay = jnp.exp(cs[:, :, -1])

    def step(h, inp):
        s_c, a_c = inp
        return a_c[..., None, None] * h + s_c, h

    h_last, h_in = lax.scan(step, h0.astype(f32).reshape(b, G, R, P, N),
                            (jnp.moveaxis(chunk_states, 1, 0), jnp.moveaxis(chunk_decay, 1, 0)))
    h_in = jnp.moveaxis(h_in, 0, 1)
    y_off = jnp.einsum('bcqgn,bcgrpn->bcqgrp', Cc, h_in) * jnp.exp(cs)[..., None]
    y = (y_diag + y_off).reshape(b, l, H, P).astype(x.dtype)
    return y, h_last.reshape(b, H, P, N)


def lin_combine(e1, e2):
    a1, b1 = e1
    a2, b2 = e2
    return a1 * a2, a2 * b1 + b2


def rg_lru(x, w_a, b_a, w_x, b_x, lam, h0):
    b, l, w = x.shape
    xf = x.astype(jnp.float32)
    xb = xf.reshape(b, l, LRU_BLOCKS, LRU_BLOCK_DIM)
    r = jax.nn.sigmoid(jnp.einsum('blki,kij->blkj', xb, w_a.astype(jnp.float32)).reshape(b, l, w) + b_a)
    i = jax.nn.sigmoid(jnp.einsum('blki,kij->blkj', xb, w_x.astype(jnp.float32)).reshape(b, l, w) + b_x)
    log_a = -LRU_C * r * jax.nn.softplus(-lam.astype(jnp.float32))
    a = jnp.exp(log_a)
    u = jnp.sqrt(-jnp.expm1(2.0 * log_a)) * (i * xf)
    u = u.at[:, 0].add(a[:, 0] * h0.astype(jnp.float32))
    _, hs = lax.associative_scan(lin_combine, (a, u), axis=1)
    return hs.astype(x.dtype), hs[:, -1]


def swiglu(h, wg, wu, wd):
    return jnp.einsum('blf,fd->bld', jax.nn.silu(jnp.einsum('bld,df->blf', h, wg)) * jnp.einsum('bld,df->blf', h, wu), wd)


def attn_mixer(h, pos, p, ai, li, past):
    b, l, _ = h.shape
    proj = jnp.einsum('bld,de->ble', h, p['attn_w_in'][ai])
    q_sb, k_sb, v_sb, q_d, k_d, v_d = jnp.split(proj, ATTN_SPLITS, axis=-1)
    q_sb = q_sb.reshape(b, l, SB_HEADS, HEAD_DIM)
    k_sb = k_sb.reshape(b, l, SB_HEADS, HEAD_DIM)
    v_sb = v_sb.reshape(b, l, SB_HEADS, HEAD_DIM)
    q_d = rope(q_d.reshape(b, l, DIFF_HEADS, 2, HEAD_DIM), pos)
    k_d = rope(k_d.reshape(b, l, DIFF_HEADS, 2, HEAD_DIM), pos)
    v_d = v_d.reshape(b, l, DIFF_HEADS, 2 * HEAD_DIM)
    lam_init = 0.8 - 0.6 * math.exp(-0.3 * li)
    lv = p['diff_lam'][ai].astype(jnp.float32)
    lam = jnp.exp(jnp.sum(lv[0] * lv[1])) - jnp.exp(jnp.sum(lv[2] * lv[3])) + lam_init
    if past is None:
        k_pos = pos
        o_sb = over_query_blocks(lambda qb, qp: sb_attend(qb, qp, (k_sb,), (v_sb,), k_pos), q_sb, pos)
        o_d = over_query_blocks(lambda qb, qp: diff_attend(qb, qp, (k_d,), (v_d,), k_pos, lam), q_d, pos)
    else:
        kp_sb, vp_sb, kp_d, vp_d = past
        k_pos = jnp.concatenate([jnp.arange(kp_sb.shape[1]), pos])
        o_sb = sb_attend(q_sb, pos, (kp_sb, k_sb), (vp_sb, v_sb), k_pos)
        o_d = diff_attend(q_d, pos, (kp_d, k_d), (vp_d, v_d), k_pos, lam)
    o_d = rmsnorm(o_d, p['diff_subln'][ai]) * (1.0 - lam_init)
    o = jnp.concatenate([o_sb.reshape(b, l, SB_WIDTH), o_d.reshape(b, l, DIFF_V)], axis=-1)
    return jnp.einsum('ble,ed->bld', o, p['attn_w_out'][ai]), (k_sb, v_sb, k_d, v_d)


def rec_mixer(h, p, ri, state):
    ssd_h0, ssd_buf, lru_h0, lru_buf = state
    b, l, _ = h.shape
    proj = jnp.einsum('bld,de->ble', h, p['rec_w_in'][ri])
    z, xbc, dt_raw, gate, xl = jnp.split(proj, REC_SPLITS, axis=-1)
    xbc, ssd_buf_new = causal_conv(xbc, ssd_buf, p['ssd_conv_w'][ri], p['ssd_conv_b'][ri])
    xbc = jax.nn.silu(xbc)
    xs, Bm, Cm = jnp.split(xbc, (SSD_WIDTH, SSD_WIDTH + SSD_GROUPS * SSD_STATE), axis=-1)
    xs = xs.reshape(b, l, SSD_HEADS, SSD_HEAD_DIM)
    Bm = Bm.reshape(b, l, SSD_GROUPS, SSD_STATE)
    Cm = Cm.reshape(b, l, SSD_GROUPS, SSD_STATE)
    dt = jax.nn.softplus(dt_raw.astype(jnp.float32) + p['ssd_dt_bias'][ri].astype(jnp.float32))
    A = -jnp.exp(p['ssd_a_log'][ri].astype(jnp.float32))
    y, ssd_h_new = ssd_scan(xs, dt, A, Bm, Cm, ssd_h0)
    y = y + p['ssd_d'][ri][:, None] * xs
    y = y.reshape(b, l, SSD_WIDTH) * jax.nn.silu(z)
    y = rmsnorm(y.reshape(b, l, SSD_GROUPS, SSD_WIDTH // SSD_GROUPS),
                p['ssd_norm'][ri].reshape(SSD_GROUPS, SSD_WIDTH // SSD_GROUPS)).reshape(b, l, SSD_WIDTH)
    xl, lru_buf_new = causal_conv(xl, lru_buf, p['lru_conv_w'][ri], p['lru_conv_b'][ri])
    hl, lru_h_new = rg_lru(xl, p['lru_w_a'][ri], p['lru_b_a'][ri], p['lru_w_x'][ri], p['lru_b_x'][ri], p['lru_lam'][ri], lru_h0)
    yl = hl * jax.nn.gelu(gate)
    o = jnp.concatenate([y, yl], axis=-1)
    return jnp.einsum('ble,ed->bld', o, p['rec_w_out'][ri]), (ssd_h_new, ssd_buf_new, lru_h_new, lru_buf_new)


def trunk(x, c, pos, p, past_fn, state_fn):
    cs = jax.nn.silu(c)
    attn_new, rec_new = [], []
    for li in range(DEPTH):
        mod = jnp.einsum('bd,de->be', cs, p['w_ada'][li]) + p['b_ada'][li]
        sh1, sc1, g1, sh2, sc2, g2 = jnp.split(mod, 6, axis=-1)
        h = modulate(x, p['norm_mix'][li], sh1, sc1)
        if li % 2 == 0:
            o, st = attn_mixer(h, pos, p, li // 2, li, past_fn(li // 2))
            attn_new.append(st)
        else:
            o, st = rec_mixer(h, p, li // 2, state_fn(li // 2))
            rec_new.append(st)
        x = x + g1[:, None, :] * o
        h = modulate(x, p['norm_ffn'][li], sh2, sc2)
        x = x + g2[:, None, :] * swiglu(h, p['ffn_w_gate'][li], p['ffn_w_up'][li], p['ffn_w_down'][li])
    sh, sc = jnp.split(jnp.einsum('bd,de->be', cs, p['w_ada_final']) + p['b_ada_final'], 2, axis=-1)
    y = modulate(x, p['norm_final'], sh, sc)
    attn_states = tuple(jnp.stack([s[j] for s in attn_new]) for j in range(4))
    rec_states = tuple(jnp.stack([s[j] for s in rec_new]) for j in range(4))
    return y, attn_states, rec_states


def setup_inputs(seed: int = 0) -> dict:
    key = jax.random.key(seed)
    keys = iter(jax.random.split(key, 64))
    f32 = jnp.float32

    def nrm(shape, scale=1.0):
        return jax.random.normal(next(keys), shape, f32) * scale

    def unif(shape, lo, hi):
        return jax.random.uniform(next(keys), shape, f32, lo, hi)

    NA, NR = N_ATTN_LAYERS, N_REC_LAYERS
    n_pages = PAST_LEN // PAGE_SIZE
    n_used = DEC_BATCH * n_pages
    n_phys = n_used + n_used // 4
    page_table = jax.random.permutation(next(keys), n_phys)[:n_used].reshape(DEC_BATCH, n_pages).astype(jnp.int32)
    dt0 = jnp.exp(unif((NR, SSD_HEADS), math.log(1e-3), math.log(1e-1)))
    a0 = unif((NR, LRU_WIDTH), 0.9, 0.999)
    s0 = a0 ** (1.0 / LRU_C)
    return {
        'x_prompt': nrm((BATCH, SEQ, D_MODEL)),
        'x_sample': nrm((DEC_BATCH, DEC_SEQ, D_MODEL)),
        'c_prompt': nrm((BATCH, D_MODEL)),
        'c_sample': nrm((DEC_BATCH, D_MODEL)),
        'cache_sb_k': nrm((NA, n_phys, PAGE_SIZE, SB_HEADS, HEAD_DIM)),
        'cache_sb_v': nrm((NA, n_phys, PAGE_SIZE, SB_HEADS, HEAD_DIM)),
        'cache_diff_k': nrm((NA, n_phys, PAGE_SIZE, DIFF_HEADS, 2, HEAD_DIM)),
        'cache_diff_v': nrm((NA, n_phys, PAGE_SIZE, DIFF_HEADS, 2 * HEAD_DIM)),
        'state_ssd': nrm((NR, DEC_BATCH, SSD_HEADS, SSD_HEAD_DIM, SSD_STATE), 0.1),
        'state_ssd_conv': nrm((NR, DEC_BATCH, CONV_WIDTH - 1, SSD_CONV_CH)),
        'state_lru': nrm((NR, DEC_BATCH, LRU_WIDTH), 0.5),
        'state_lru_conv': nrm((NR, DEC_BATCH, CONV_WIDTH - 1, LRU_WIDTH)),
        'page_table': page_table,
        'w_ada': nrm((DEPTH, D_MODEL, 6 * D_MODEL), 0.5 * D_MODEL ** -0.5),
        'b_ada': nrm((DEPTH, 6 * D_MODEL), 0.02),
        'norm_mix': 1.0 + nrm((DEPTH, D_MODEL), 0.02),
        'norm_ffn': 1.0 + nrm((DEPTH, D_MODEL), 0.02),
        'attn_w_in': nrm((NA, D_MODEL, ATTN_IN), D_MODEL ** -0.5),
        'attn_w_out': nrm((NA, ATTN_OUT, D_MODEL), ATTN_OUT ** -0.5),
        'diff_lam': nrm((NA, 4, HEAD_DIM), 0.1),
        'diff_subln': 1.0 + nrm((NA, 2 * HEAD_DIM), 0.02),
        'rec_w_in': nrm((NR, D_MODEL, REC_IN), D_MODEL ** -0.5),
        'ssd_conv_w': nrm((NR, CONV_WIDTH, SSD_CONV_CH), CONV_WIDTH ** -0.5),
        'ssd_conv_b': nrm((NR, SSD_CONV_CH), 0.02),
        'ssd_dt_bias': dt0 + jnp.log(-jnp.expm1(-dt0)),
        'ssd_a_log': jnp.log(unif((NR, SSD_HEADS), 1.0, 16.0)),
        'ssd_d': 1.0 + nrm((NR, SSD_HEADS), 0.1),
        'ssd_norm': 1.0 + nrm((NR, SSD_WIDTH), 0.02),
        'lru_conv_w': nrm((NR, CONV_WIDTH, LRU_WIDTH), CONV_WIDTH ** -0.5),
        'lru_conv_b': nrm((NR, LRU_WIDTH), 0.02),
        'lru_w_a': nrm((NR, LRU_BLOCKS, LRU_BLOCK_DIM, LRU_BLOCK_DIM), LRU_BLOCK_DIM ** -0.5),
        'lru_b_a': nrm((NR, LRU_WIDTH), 0.02),
        'lru_w_x': nrm((NR, LRU_BLOCKS, LRU_BLOCK_DIM, LRU_BLOCK_DIM), LRU_BLOCK_DIM ** -0.5),
        'lru_b_x': nrm((NR, LRU_WIDTH), 0.02),
        'lru_lam': jnp.log(s0) - jnp.log1p(-s0),
        'rec_w_out': nrm((NR, REC_OUT, D_MODEL), REC_OUT ** -0.5),
        'ffn_w_gate': nrm((DEPTH, D_MODEL, FFN_HIDDEN), D_MODEL ** -0.5),
        'ffn_w_up': nrm((DEPTH, D_MODEL, FFN_HIDDEN), D_MODEL ** -0.5),
        'ffn_w_down': nrm((DEPTH, FFN_HIDDEN, D_MODEL), FFN_HIDDEN ** -0.5),
        'w_ada_final': nrm((D_MODEL, 2 * D_MODEL), 0.5 * D_MODEL ** -0.5),
        'b_ada_final': nrm((2 * D_MODEL,), 0.02),
        'norm_final': 1.0 + nrm((D_MODEL,), 0.02),
    }


def reference(x_prompt, x_sample, c_prompt, c_sample, cache_sb_k, cache_sb_v, cache_diff_k, cache_diff_v,
              state_ssd, state_ssd_conv, state_lru, state_lru_conv, page_table,
              w_ada, b_ada, norm_mix, norm_ffn, attn_w_in, attn_w_out, diff_lam, diff_subln,
              rec_w_in, ssd_conv_w, ssd_conv_b, ssd_dt_bias, ssd_a_log, ssd_d, ssd_norm,
              lru_conv_w, lru_conv_b, lru_w_a, lru_b_a, lru_w_x, lru_b_x, lru_lam, rec_w_out,
              ffn_w_gate, ffn_w_up, ffn_w_down, w_ada_final, b_ada_final, norm_final):
    p = dict(w_ada=w_ada, b_ada=b_ada, norm_mix=norm_mix, norm_ffn=norm_ffn,
             attn_w_in=attn_w_in, attn_w_out=attn_w_out, diff_lam=diff_lam, diff_subln=diff_subln,
             rec_w_in=rec_w_in, ssd_conv_w=ssd_conv_w, ssd_conv_b=ssd_conv_b, ssd_dt_bias=ssd_dt_bias,
             ssd_a_log=ssd_a_log, ssd_d=ssd_d, ssd_norm=ssd_norm, lru_conv_w=lru_conv_w, lru_conv_b=lru_conv_b,
             lru_w_a=lru_w_a, lru_b_a=lru_b_a, lru_w_x=lru_w_x, lru_b_x=lru_b_x, lru_lam=lru_lam,
             rec_w_out=rec_w_out, ffn_w_gate=ffn_w_gate, ffn_w_up=ffn_w_up, ffn_w_down=ffn_w_down,
             w_ada_final=w_ada_final, b_ada_final=b_ada_final, norm_final=norm_final)

    bp, lp = x_prompt.shape[0], x_prompt.shape[1]
    pos_p = jnp.arange(lp)

    def prompt_past(ai):
        return None

    def prompt_state(ri):
        return (jnp.zeros((bp, SSD_HEADS, SSD_HEAD_DIM, SSD_STATE), jnp.float32),
                jnp.zeros((bp, CONV_WIDTH - 1, SSD_CONV_CH), x_prompt.dtype),
                jnp.zeros((bp, LRU_WIDTH), jnp.float32),
                jnp.zeros((bp, CONV_WIDTH - 1, LRU_WIDTH), x_prompt.dtype))

    y_prompt, attn_p, rec_p = trunk(x_prompt, c_prompt, pos_p, p, prompt_past, prompt_state)
    sb_k_p, sb_v_p, diff_k_p, diff_v_p = attn_p
    ssd_p, ssd_conv_p, lru_p, lru_conv_p = rec_p

    bs, ls = x_sample.shape[0], x_sample.shape[1]
    past_len = page_table.shape[1] * PAGE_SIZE
    pos_s = past_len + jnp.arange(ls)

    def sample_past(ai):
        def gather(cache):
            return cache[ai, page_table].reshape((bs, past_len) + cache.shape[3:])
        return (gather(cache_sb_k), gather(cache_sb_v), gather(cache_diff_k), gather(cache_diff_v))

    def sample_state(ri):
        return (state_ssd[ri], state_ssd_conv[ri], state_lru[ri], state_lru_conv[ri])

    y_sample, attn_s, rec_s = trunk(x_sample, c_sample, pos_s, p, sample_past, sample_state)
    sb_k_s, sb_v_s, diff_k_s, diff_v_s = attn_s
    ssd_s, ssd_conv_s, lru_s, lru_conv_s = rec_s

    return (y_prompt, y_sample,
            sb_k_p, sb_v_p, diff_k_p, diff_v_p, ssd_p, ssd_conv_p, lru_p, lru_conv_p,
            sb_k_s, sb_v_s, diff_k_s, diff_v_s, ssd_s, ssd_conv_s, lru_s, lru_conv_s)
```

```python
import functools
import math

import jax
import jax.numpy as jnp
from jax import lax
from jax.experimental import pallas as pl
from jax.experimental.pallas import tpu as pltpu

BF = jnp.bfloat16
F32 = jnp.float32

D_MODEL = 1024
DEPTH = 4
PAGE = 128
HEAD_DIM = 64
SB_HEADS = 8
SB_WIDTH = SB_HEADS * HEAD_DIM
DIFF_HEADS = 4
ATTN_IN = 6 * SB_WIDTH
ROPE_THETA = 10000.0
NEG_INF = -1e30
SSD_HEADS = 16
SSD_WIDTH = 1024
SSD_GROUPS = 4
SSD_STATE = 128
CONV_WIDTH = 4
LRU_WIDTH = 1024
LRU_BLOCKS = 16
LRU_C = 8.0
FFN_HIDDEN = 2816
EPS = 1e-6

LANES = 128
ROW_TILE = 512
ATT_TILE = 128
CHUNK = 128
FFN_TILE = 1408
VMEM_LIMIT = 52 * 1024 * 1024


def _cparams(*sem):
    return pltpu.CompilerParams(dimension_semantics=sem, vmem_limit_bytes=VMEM_LIMIT)


def _dot(a, b):
    return jnp.dot(a, b, preferred_element_type=F32)


def _dot_nt(a, b):
    return lax.dot_general(a, b, (((1,), (1,)), ((), ())), preferred_element_type=F32)


def _softplus(x):
    return jnp.maximum(x, 0.0) + jnp.log1p(jnp.exp(-jnp.abs(x)))


def _log_sigmoid(z):
    return jnp.minimum(z, 0.0) - jnp.log1p(jnp.exp(-jnp.abs(z)))


def _modulate(x, g, sh, sc):
    y = x * lax.rsqrt(jnp.mean(x * x, axis=-1, keepdims=True) + EPS) * g
    return y * (1.0 + sc) + sh


def _split3(x):
    a = x.astype(BF)
    r = x - a.astype(F32)
    b = r.astype(BF)
    c = (r - b.astype(F32)).astype(BF)
    return a, b, c


def _split2(x):
    a = x.astype(BF)
    b = (x - a.astype(F32)).astype(BF)
    return a, b


def _ada_body(c_ref, w_ref, b_ref, o_ref):
    h = jax.nn.silu(c_ref[...]).astype(BF)
    o_ref[...] = _dot(h, w_ref[...].astype(BF)) + b_ref[...]


def _ada_call(c_all, w, b):
    nl, d, n = w.shape
    mp = c_all.shape[0]
    tn = 1024
    return pl.pallas_call(
        _ada_body,
        grid=(nl, n // tn),
        in_specs=[pl.BlockSpec((mp, d), lambda l, j: (0, 0)),
                  pl.BlockSpec((None, d, tn), lambda l, j: (l, 0, j)),
                  pl.BlockSpec((None, 1, tn), lambda l, j: (l, 0, j))],
        out_specs=pl.BlockSpec((None, mp, tn), lambda l, j: (l, 0, j)),
        out_shape=jax.ShapeDtypeStruct((nl, mp, n), F32),
        compiler_params=_cparams("arbitrary", "arbitrary"),
        name="ada",
    )(c_all, w, b)


class _Group:
    def __init__(self, batch, length, per_row):
        self.batch = batch
        self.length = length
        self.rows = batch * length
        self.per_row = per_row
        self.tm = min(ROW_TILE, self.rows)
        self.blocks_per_seq = 1 if per_row else length // self.tm

    def mod_spec(self, d=D_MODEL):
        if self.per_row:
            return pl.BlockSpec((self.tm, d), lambda i, *_: (i, 0))
        bps = self.blocks_per_seq
        return pl.BlockSpec((None, 1, d), lambda i, *_: (i // bps, 0, 0))

    def pos_spec(self):
        if self.per_row:
            return pl.BlockSpec((self.tm, LANES), lambda i, *_: (i, 0))
        bps = self.blocks_per_seq
        return pl.BlockSpec((self.tm, LANES), lambda i, *_: (i % bps, 0))


def _row_spec(tm, n):
    return pl.BlockSpec((tm, n), lambda i, *_: (i, 0))


def _full_spec(shape):
    nd = len(shape)
    return pl.BlockSpec(shape, lambda *_: (0,) * nd)


def _rope(x, cos, sin):
    n = x.shape[1] // LANES
    c = jnp.concatenate([cos] * n, axis=1)
    s = jnp.concatenate([sin] * n, axis=1)
    lane = lax.broadcasted_iota(jnp.int32, x.shape, 1)
    first = (lane % HEAD_DIM) < (HEAD_DIM // 2)
    w = x.shape[1]
    partner = jnp.where(first, pltpu.roll(x, w - HEAD_DIM // 2, 1), pltpu.roll(x, HEAD_DIM // 2, 1))
    return x * c + partner * s


def _attn_in_body(x_ref, g_ref, sh_ref, sc_ref, w_ref, cos_ref, sin_ref,
                  qsb_ref, ksb_ref, ksbh_ref, vsb_ref, vsbh_ref, qd_ref, kd_ref, kdh_ref, vd_ref, vdh_ref):
    h = _modulate(x_ref[...], g_ref[...], sh_ref[...], sc_ref[...]).astype(BF)
    proj = _dot(h, w_ref[...])
    w = SB_WIDTH
    scale = HEAD_DIM ** -0.5
    qsb_ref[...] = (proj[:, 0:w] * scale).astype(BF)
    k = proj[:, w:2 * w]
    ksb_ref[...] = k
    ksbh_ref[...] = k.astype(BF)
    v = proj[:, 2 * w:3 * w]
    vsb_ref[...] = v
    vsbh_ref[...] = v.astype(BF)
    cos = cos_ref[...]
    sin = sin_ref[...]
    qd_ref[...] = (_rope(proj[:, 3 * w:4 * w], cos, sin) * scale).astype(BF)
    kd = _rope(proj[:, 4 * w:5 * w], cos, sin)
    kd_ref[...] = kd
    kdh_ref[...] = kd.astype(BF)
    vd = proj[:, 5 * w:6 * w]
    vd_ref[...] = vd
    vdh_ref[...] = vd.astype(BF)


def _attn_in_call(grp, x, g, sh, sc, w, cos, sin):
    m, tm = grp.rows, grp.tm
    o32 = jax.ShapeDtypeStruct((m, SB_WIDTH), F32)
    o16 = jax.ShapeDtypeStruct((m, SB_WIDTH), BF)
    spec = _row_spec(tm, SB_WIDTH)
    return pl.pallas_call(
        _attn_in_body,
        grid=(m // tm,),
        in_specs=[_row_spec(tm, D_MODEL), _full_spec((1, D_MODEL)), grp.mod_spec(), grp.mod_spec(),
                  _full_spec((D_MODEL, ATTN_IN)), grp.pos_spec(), grp.pos_spec()],
        out_specs=[spec] * 10,
        out_shape=[o16, o32, o16, o32, o16, o16, o32, o16, o32, o16],
        compiler_params=_cparams("arbitrary"),
        name="attn_in",
    )(x, g, sh, sc, w, cos, sin)


def _sb_prompt_body(q_ref, k_ref, v_ref, o_ref):
    t = ATT_TILE
    qi = pl.program_id(2)
    q = q_ref[...]
    lane = lax.broadcasted_iota(jnp.int32, (t, LANES), 1)
    zero = jnp.zeros_like(q)
    qs = (jnp.where(lane < HEAD_DIM, q, zero), jnp.where(lane >= HEAD_DIM, q, zero))
    row = lax.broadcasted_iota(jnp.int32, (t, t), 0)
    col = lax.broadcasted_iota(jnp.int32, (t, t), 1)
    later = (row > col).astype(BF)

    def tile(kb, carry, masked):
        start = pl.multiple_of(kb * t, t)
        k = k_ref[pl.ds(start, t), :]
        v = v_ref[pl.ds(start, t), :]
        out = []
        for qh, (run, acc) in zip(qs, carry):
            z = _dot_nt(qh, k)
            lb = _log_sigmoid(z)
            lk = lb - z
            if masked:
                valid = col < row
                lk = jnp.where(valid, lk, 0.0)
            hi, lo = _split2(lk)
            rest = _dot(hi, later) + _dot(lo, later) + run
            wgt = jnp.exp(lb + rest)
            if masked:
                wgt = jnp.where(valid, wgt, 0.0)
            acc = acc + _dot(wgt.astype(BF), v)
            run = run + jnp.sum(lk, axis=1, keepdims=True)
            out.append((run, acc))
        return tuple(out)

    init = tuple((jnp.zeros((t, 1), F32), jnp.zeros((t, LANES), F32)) for _ in range(2))
    carry = tile(qi, init, True)
    carry = lax.fori_loop(0, qi, lambda i, c: tile(qi - 1 - i, c, False), carry)
    o_ref[...] = jnp.where(lane < HEAD_DIM, carry[0][1], carry[1][1]).astype(BF)


def _prompt_attn_specs(batch, length):
    t = ATT_TILE
    nq = length // t
    q_spec = pl.BlockSpec((t, LANES), lambda b, p, i: (b * nq + i, p))
    kv_spec = pl.BlockSpec((length, LANES), lambda b, p, i: (b, p))
    return nq, q_spec, kv_spec


def _sb_prompt_call(batch, length, q, k, v):
    nq, q_spec, kv_spec = _prompt_attn_specs(batch, length)
    return pl.pallas_call(
        _sb_prompt_body,
        grid=(batch, SB_WIDTH // LANES, nq),
        in_specs=[q_spec, kv_spec, kv_spec],
        out_specs=q_spec,
        out_shape=jax.ShapeDtypeStruct(q.shape, BF),
        compiler_params=_cparams("arbitrary", "arbitrary", "arbitrary"),
        name="sb_prompt",
    )(q, k, v)


def _diff_lambda(lam_ref, lam_init):
    lv = lam_ref[...]
    s1 = jnp.sum(lv[0:1] * lv[1:2], axis=1, keepdims=True)
    s2 = jnp.sum(lv[2:3] * lv[3:4], axis=1, keepdims=True)
    return jnp.exp(s1) - jnp.exp(s2) + lam_init


def _diff_finish(o0, o1, lam, gain, lam_init):
    o = o0 - lam * o1
    y = o * lax.rsqrt(jnp.mean(o * o, axis=-1, keepdims=True) + EPS) * gain
    return y * (1.0 - lam_init)


def _diff_prompt_body(q_ref, k_ref, v_ref, lam_ref, gain_ref, o_ref, *, lam_init):
    t = ATT_TILE
    qi = pl.program_id(2)
    q = q_ref[...]
    lane = lax.broadcasted_iota(jnp.int32, (t, LANES), 1)
    zero = jnp.zeros_like(q)
    qs = (jnp.where(lane < HEAD_DIM, q, zero), jnp.where(lane >= HEAD_DIM, q, zero))
    row = lax.broadcasted_iota(jnp.int32, (t, t), 0)
    col = lax.broadcasted_iota(jnp.int32, (t, t), 1)

    def tile(kb, carry, masked):
        start = pl.multiple_of(kb * t, t)
        k = k_ref[pl.ds(start, t), :]
        v = v_ref[pl.ds(start, t), :]
        out = []
        for qh, (mx, den, acc) in zip(qs, carry):
            z = _dot_nt(qh, k)
            if masked:
                z = jnp.where(col <= row, z, NEG_INF)
            mx_new = jnp.maximum(mx, jnp.max(z, axis=1, keepdims=True))
            alpha = jnp.exp(mx - mx_new)
            p = jnp.exp(z - mx_new)
            den = alpha * den + jnp.sum(p, axis=1, keepdims=True)
            acc = alpha * acc + _dot(p.astype(BF), v)
            out.append((mx_new, den, acc))
        return tuple(out)

    init = tuple((jnp.full((t, 1), NEG_INF, F32), jnp.zeros((t, 1), F32), jnp.zeros((t, LANES), F32))
                 for _ in range(2))
    carry = tile(qi, init, True)
    carry = lax.fori_loop(0, qi, lambda i, c: tile(qi - 1 - i, c, False), carry)
    lam = _diff_lambda(lam_ref, lam_init)
    o0 = carry[0][2] / carry[0][1]
    o1 = carry[1][2] / carry[1][1]
    o_ref[...] = _diff_finish(o0, o1, lam, gain_ref[...], lam_init).astype(BF)


def _diff_prompt_call(batch, length, q, k, v, lam, gain, lam_init):
    nq, q_spec, kv_spec = _prompt_attn_specs(batch, length)
    return pl.pallas_call(
        functools.partial(_diff_prompt_body, lam_init=lam_init),
        grid=(batch, DIFF_HEADS, nq),
        in_specs=[q_spec, kv_spec, kv_spec, _full_spec((4, HEAD_DIM)), _full_spec((1, LANES))],
        out_specs=q_spec,
        out_shape=jax.ShapeDtypeStruct(q.shape, BF),
        compiler_params=_cparams("arbitrary", "arbitrary", "arbitrary"),
        name="diff_prompt",
    )(q, k, v, lam, gain)


N_CLS = 8
N_QPAD = 8
PAGES_PER_STEP = 4
Q_ROWS = N_CLS * N_QPAD


def _page_specs(n_pages, base, block):
    def spec(slot):
        def idx(b, j, pt_ref):
            page = n_pages - jnp.maximum(j, 1) * PAGES_PER_STEP + slot
            return (base + pt_ref[b * n_pages + page],) + (0,) * (len(block) - 1)
        return pl.BlockSpec(block, idx)
    return [spec(s) for s in range(PAGES_PER_STEP)]


def _class_scores(q, k_ref):
    return jnp.concatenate(
        [_dot(q[c * N_QPAD:(c + 1) * N_QPAD], k_ref[c].astype(BF)) for c in range(N_CLS)], axis=0)


def _sb_sample_body(pt_ref, q_ref, *refs, n_steps):
    np_ = PAGES_PER_STEP
    kc, vc = refs[0:np_], refs[np_:2 * np_]
    kn_ref, vn_ref, o_ref, run_s, acc_s = refs[2 * np_:]
    j = pl.program_id(1)
    q = q_ref[...]
    row = lax.broadcasted_iota(jnp.int32, (PAGE, PAGE), 0)
    col = lax.broadcasted_iota(jnp.int32, (PAGE, PAGE), 1)
    later = (row > col).astype(BF)
    tok = lax.broadcasted_iota(jnp.int32, (Q_ROWS, PAGE), 1)
    qpos = lax.broadcasted_iota(jnp.int32, (Q_ROWS, PAGE), 0) % N_QPAD

    def page(k_ref, v_ref, new):
        z = _class_scores(q, k_ref)
        lb = _log_sigmoid(z)
        lk = lb - z
        if new:
            valid = tok < qpos
            lk = jnp.where(valid, lk, 0.0)
        hi, lo = _split2(lk)
        rest = _dot(hi, later) + _dot(lo, later) + run_s[...]
        wgt = jnp.exp(lb + rest)
        if new:
            wgt = jnp.where(valid, wgt, 0.0)
        w16 = wgt.astype(BF)
        acc_s[...] += jnp.concatenate(
            [_dot_nt(w16[c * N_QPAD:(c + 1) * N_QPAD], v_ref[c].astype(BF)) for c in range(N_CLS)], axis=0)
        run_s[...] += jnp.sum(lk, axis=1, keepdims=True)

    @pl.when(j == 0)
    def _():
        run_s[...] = jnp.zeros_like(run_s)
        acc_s[...] = jnp.zeros_like(acc_s)
        page(kn_ref, vn_ref, True)

    @pl.when(j > 0)
    def _():
        for slot in reversed(range(np_)):
            page(kc[slot], vc[slot], False)

    @pl.when(j == n_steps - 1)
    def _():
        o_ref[...] = acc_s[...]


def _sb_sample_call(pt_flat, n_pages, base, q, kc, vc, kn, vn):
    nb = q.shape[0]
    n_steps = n_pages // PAGES_PER_STEP + 1
    block = (None, N_CLS, HEAD_DIM, PAGE)
    per_b = lambda *s: pl.BlockSpec((None,) + s, lambda b, j, pt: (b,) + (0,) * len(s))
    k_specs = _page_specs(n_pages, base, block)
    return pl.pallas_call(
        functools.partial(_sb_sample_body, n_steps=n_steps),
        grid_spec=pltpu.PrefetchScalarGridSpec(
            num_scalar_prefetch=1,
            grid=(nb, n_steps),
            in_specs=[per_b(Q_ROWS, HEAD_DIM)] + k_specs + k_specs
                     + [per_b(N_CLS, HEAD_DIM, PAGE), per_b(N_CLS, HEAD_DIM, PAGE)],
            out_specs=per_b(Q_ROWS, HEAD_DIM),
            scratch_shapes=[pltpu.VMEM((Q_ROWS, 1), F32), pltpu.VMEM((Q_ROWS, HEAD_DIM), F32)]),
        out_shape=jax.ShapeDtypeStruct((nb, Q_ROWS, HEAD_DIM), F32),
        compiler_params=_cparams("arbitrary", "arbitrary"),
        name="sb_sample",
    )(pt_flat, q, *([kc] * PAGES_PER_STEP), *([vc] * PAGES_PER_STEP), kn, vn)


def _diff_sample_body(pt_ref, q_ref, *refs, n_steps, lam_init):
    np_ = PAGES_PER_STEP
    kc, vc = refs[0:np_], refs[np_:2 * np_]
    kn_ref, vn_ref, lam_ref, gain_ref, o_ref, mx_s, den_s, acc_s = refs[2 * np_:]
    j = pl.program_id(1)
    q = q_ref[...]
    tok = lax.broadcasted_iota(jnp.int32, (Q_ROWS, PAGE), 1)
    qpos = lax.broadcasted_iota(jnp.int32, (Q_ROWS, PAGE), 0) % N_QPAD
    rows_per_head = 2 * N_QPAD

    def page(k_ref, v_ref, new):
        z = _class_scores(q, k_ref)
        if new:
            z = jnp.where(tok <= qpos, z, NEG_INF)
        mx = mx_s[...]
        mx_new = jnp.maximum(mx, jnp.max(z, axis=1, keepdims=True))
        alpha = jnp.exp(mx - mx_new)
        p = jnp.exp(z - mx_new)
        den_s[...] = alpha * den_s[...] + jnp.sum(p, axis=1, keepdims=True)
        mx_s[...] = mx_new
        p16 = p.astype(BF)
        pv = jnp.concatenate(
            [_dot(p16[h * rows_per_head:(h + 1) * rows_per_head],
                  v_ref[pl.ds(h, PAGE, stride=DIFF_HEADS), :].astype(BF)) for h in range(DIFF_HEADS)], axis=0)
        acc_s[...] = alpha * acc_s[...] + pv

    @pl.when(j == 0)
    def _():
        mx_s[...] = jnp.full_like(mx_s, NEG_INF)
        den_s[...] = jnp.zeros_like(den_s)
        acc_s[...] = jnp.zeros_like(acc_s)
        page(kn_ref, vn_ref, True)

    @pl.when(j > 0)
    def _():
        for slot in reversed(range(np_)):
            page(kc[slot], vc[slot], False)

    @pl.when(j == n_steps - 1)
    def _():
        o = acc_s[...] / den_s[...]
        o0 = jnp.concatenate([o[h * rows_per_head:h * rows_per_head + N_QPAD] for h in range(DIFF_HEADS)], axis=0)
        o1 = jnp.concatenate([o[h * rows_per_head + N_QPAD:(h + 1) * rows_per_head] for h in range(DIFF_HEADS)],
                             axis=0)
        lam = _diff_lambda(lam_ref, lam_init)
        o_ref[...] = _diff_finish(o0, o1, lam, gain_ref[...], lam_init)


def _diff_sample_call(pt_flat, n_pages, base, q, kc, vc, kn, vn, lam, gain, lam_init):
    nb = q.shape[0]
    n_steps = n_pages // PAGES_PER_STEP + 1
    v_rows = PAGE * DIFF_HEADS
    per_b = lambda *s: pl.BlockSpec((None,) + s, lambda b, j, pt: (b,) + (0,) * len(s))
    const = lambda shape: pl.BlockSpec(shape, lambda b, j, pt: (0,) * len(shape))
    k_specs = _page_specs(n_pages, base, (None, N_CLS, HEAD_DIM, PAGE))
    v_specs = _page_specs(n_pages, base, (None, v_rows, LANES))
    return pl.pallas_call(
        functools.partial(_diff_sample_body, n_steps=n_steps, lam_init=lam_init),
        grid_spec=pltpu.PrefetchScalarGridSpec(
            num_scalar_prefetch=1,
            grid=(nb, n_steps),
            in_specs=[per_b(Q_ROWS, HEAD_DIM)] + k_specs + v_specs
                     + [per_b(N_CLS, HEAD_DIM, PAGE), per_b(v_rows, LANES), const((4, HEAD_DIM)), const((1, LANES))],
            out_specs=per_b(Q_ROWS // 2, LANES),
            scratch_shapes=[pltpu.VMEM((Q_ROWS, 1), F32), pltpu.VMEM((Q_ROWS, 1), F32),
                            pltpu.VMEM((Q_ROWS, LANES), F32)]),
        out_shape=jax.ShapeDtypeStruct((nb, Q_ROWS // 2, LANES), F32),
        compiler_params=_cparams("arbitrary", "arbitrary"),
        name="diff_sample",
    )(pt_flat, q, *([kc] * PAGES_PER_STEP), *([vc] * PAGES_PER_STEP), kn, vn, lam, gain)


def _out_proj_body(a_ref, b_ref, w_ref, x_ref, g_ref, o_ref):
    ka = a_ref.shape[1]
    y = _dot(a_ref[...], w_ref[0:ka, :]) + _dot(b_ref[...], w_ref[ka:, :])
    o_ref[...] = x_ref[...] + g_ref[...] * y


def _out_proj_call(grp, a, b, w, x, g):
    m, tm = grp.rows, grp.tm
    return pl.pallas_call(
        _out_proj_body,
        grid=(m // tm,),
        in_specs=[_row_spec(tm, a.shape[1]), _row_spec(tm, b.shape[1]), _full_spec(w.shape),
                  _row_spec(tm, D_MODEL), grp.mod_spec()],
        out_specs=_row_spec(tm, D_MODEL),
        out_shape=jax.ShapeDtypeStruct((m, D_MODEL), F32),
        compiler_params=_cparams("arbitrary"),
        name="out_proj",
    )(a, b, w, x, g)


def _ffn_body(x_ref, gn_ref, sh_ref, sc_ref, g2_ref, wg_ref, wu_ref, wd_ref, o_ref, h_s, acc_s):
    j = pl.program_id(1)

    @pl.when(j == 0)
    def _():
        h_s[...] = _modulate(x_ref[...], gn_ref[...], sh_ref[...], sc_ref[...]).astype(BF)
        acc_s[...] = jnp.zeros_like(acc_s)

    h = h_s[...]
    a = jax.nn.silu(_dot(h, wg_ref[...])) * _dot(h, wu_ref[...])
    acc_s[...] += _dot(a.astype(BF), wd_ref[...])

    @pl.when(j == pl.num_programs(1) - 1)
    def _():
        o_ref[...] = x_ref[...] + g2_ref[...] * acc_s[...]


def _ffn_call(grp, x, gn, sh, sc, g2, wg, wu, wd):
    m, tm = grp.rows, grp.tm
    tf = FFN_TILE
    return pl.pallas_call(
        _ffn_body,
        grid=(m // tm, FFN_HIDDEN // tf),
        in_specs=[_row_spec(tm, D_MODEL), _full_spec((1, D_MODEL)), grp.mod_spec(), grp.mod_spec(), grp.mod_spec(),
                  pl.BlockSpec((D_MODEL, tf), lambda i, j: (0, j)),
                  pl.BlockSpec((D_MODEL, tf), lambda i, j: (0, j)),
                  pl.BlockSpec((tf, D_MODEL), lambda i, j: (j, 0))],
        out_specs=_row_spec(tm, D_MODEL),
        out_shape=jax.ShapeDtypeStruct((m, D_MODEL), F32),
        scratch_shapes=[pltpu.VMEM((tm, D_MODEL), BF), pltpu.VMEM((tm, D_MODEL), F32)],
        compiler_params=_cparams("arbitrary", "arbitrary"),
        name="ffn",
    )(x, gn, sh, sc, g2, wg, wu, wd)


def _final_body(x_ref, g_ref, sh_ref, sc_ref, o_ref):
    o_ref[...] = _modulate(x_ref[...], g_ref[...], sh_ref[...], sc_ref[...])


def _final_call(grp, x, g, sh, sc):
    m, tm = grp.rows, grp.tm
    return pl.pallas_call(
        _final_body,
        grid=(m // tm,),
        in_specs=[_row_spec(tm, D_MODEL), _full_spec((1, D_MODEL)), grp.mod_spec(), grp.mod_spec()],
        out_specs=_row_spec(tm, D_MODEL),
        out_shape=jax.ShapeDtypeStruct((m, D_MODEL), F32),
        compiler_params=_cparams("arbitrary"),
        name="final_norm",
    )(x, g, sh, sc)


REC_MAIN = 5 * 1024


def _rec_in_body(x_ref, g_ref, sh_ref, sc_ref, w_ref, wdt_ref, o_ref, dt_ref, h_s):
    j = pl.program_id(1)

    @pl.when(j == 0)
    def _():
        h = _modulate(x_ref[...], g_ref[...], sh_ref[...], sc_ref[...]).astype(BF)
        h_s[...] = h
        dt_ref[...] = _dot(h, wdt_ref[...])

    o_ref[...] = _dot(h_s[...], w_ref[...])


def _rec_in_call(grp, x, g, sh, sc, w_main, w_dt):
    m, tm = grp.rows, grp.tm
    tn = 1024
    return pl.pallas_call(
        _rec_in_body,
        grid=(m // tm, REC_MAIN // tn),
        in_specs=[_row_spec(tm, D_MODEL), _full_spec((1, D_MODEL)), grp.mod_spec(), grp.mod_spec(),
                  pl.BlockSpec((D_MODEL, tn), lambda i, j: (0, j)), _full_spec((D_MODEL, LANES))],
        out_specs=[pl.BlockSpec((tm, tn), lambda i, j: (i, j)), _row_spec(tm, LANES)],
        out_shape=[jax.ShapeDtypeStruct((m, REC_MAIN), F32), jax.ShapeDtypeStruct((m, LANES), F32)],
        scratch_shapes=[pltpu.VMEM((tm, D_MODEL), BF)],
        compiler_params=_cparams("arbitrary", "arbitrary"),
        name="rec_in",
    )(x, g, sh, sc, w_main, w_dt)


def _causal_conv(ext_ref, x, buf_ref, w_ref, b_ref, first):
    t = x.shape[0]

    @pl.when(first)
    def _():
        ext_ref[0:8, :] = buf_ref[...]

    ext_ref[8:8 + t, :] = x
    y = b_ref[...] + x * w_ref[CONV_WIDTH - 1:CONV_WIDTH, :]
    for back in range(1, CONV_WIDTH):
        k = CONV_WIDTH - 1 - back
        y = y + ext_ref[pl.ds(8 - back, t), :] * w_ref[k:k + 1, :]
    ext_ref[0:8, :] = ext_ref[t:t + 8, :]
    return y


def _pair_expand(v, npairs):
    rows = v.shape[0]
    lane = lax.broadcasted_iota(jnp.int32, (rows, LANES), 1)
    parts = [jnp.where(lane < HEAD_DIM, v[:, 2 * p:2 * p + 1], v[:, 2 * p + 1:2 * p + 2]) for p in range(npairs)]
    return jnp.concatenate(parts, axis=1)


def _ssd_body(z_ref, xs_ref, bc_ref, dt_ref, h0_ref, bufx_ref, bufbc_ref, cwx_ref, cbx_ref, cwbc_ref, cbbc_ref,
              dtb_ref, alog_ref, dexp_ref, nw_ref, y_ref, hout_ref, extx, extbc, hst):
    q = CHUNK
    c = pl.program_id(1)
    first = c == 0

    @pl.when(first)
    def _():
        hst[...] = h0_ref[...]

    xs = jax.nn.silu(_causal_conv(extx, xs_ref[...], bufx_ref, cwx_ref, cbx_ref, first))
    bc = jax.nn.silu(_causal_conv(extbc, bc_ref[...], bufbc_ref, cwbc_ref, cbbc_ref, first))
    gw = SSD_GROUPS * SSD_STATE
    bm = bc[:, 0:gw].astype(BF)
    cm = bc[:, gw:2 * gw].astype(BF)

    dt = _softplus(dt_ref[...] + dtb_ref[...])
    a = dt * (-jnp.exp(alog_ref[...]))
    row = lax.broadcasted_iota(jnp.int32, (q, q), 0)
    col = lax.broadcasted_iota(jnp.int32, (q, q), 1)
    causal = col <= row
    tri = causal.astype(BF)
    a1, a2, a3 = _split3(a)
    cs = _dot(tri, a1) + _dot(tri, a2) + _dot(tri, a3)
    cs_t = cs.T
    cs_last = cs[q - 1:q, :]
    dend = jnp.exp(cs_last - cs)
    ecs = jnp.exp(cs)
    chunk_decay_t = jnp.exp(cs_t[:, q - 1:q])

    npairs = SSD_HEADS // 2
    xdt = xs * _pair_expand(dt, npairs)
    xdt16 = xdt.astype(BF)
    xd_t = (xdt * _pair_expand(dend, npairs)).T.astype(BF)
    ecs_x = _pair_expand(ecs, npairs)
    lane = lax.broadcasted_iota(jnp.int32, (q, LANES), 1)
    srow = lax.broadcasted_iota(jnp.int32, (LANES, SSD_STATE), 0)
    heads_per_group = SSD_HEADS // SSD_GROUPS
    ys = []
    for p in range(npairs):
        g = (2 * p) // heads_per_group
        bg = bm[:, g * SSD_STATE:(g + 1) * SSD_STATE]
        cg = cm[:, g * SSD_STATE:(g + 1) * SSD_STATE]
        cb = _dot_nt(cg, bg)
        xp = xdt16[:, p * LANES:(p + 1) * LANES]
        zero = jnp.zeros_like(xp)
        y = None
        for hh in range(2):
            h = 2 * p + hh
            seg = cs[:, h:h + 1] - cs_t[h:h + 1, :]
            decay = jnp.where(causal, jnp.exp(jnp.where(causal, seg, 0.0)), 0.0)
            xh = jnp.where((lane < HEAD_DIM) == (hh == 0), xp, zero)
            part = _dot((cb * decay).astype(BF), xh)
            y = part if y is None else y + part
        hp = hst[p]
        y = y + _dot_nt(cg, hp.astype(BF)) * ecs_x[:, p * LANES:(p + 1) * LANES]
        states = _dot(xd_t[p * LANES:(p + 1) * LANES, :], bg)
        cdec = jnp.where(srow < HEAD_DIM, chunk_decay_t[2 * p:2 * p + 1, :], chunk_decay_t[2 * p + 1:2 * p + 2, :])
        hst[p] = hp * cdec + states
        ys.append(y)
    y = jnp.concatenate(ys, axis=1)
    y = y + dexp_ref[...] * xs
    y = y * jax.nn.silu(z_ref[...])
    gsz = SSD_WIDTH // SSD_GROUPS
    outs = []
    for g in range(SSD_GROUPS):
        yg = y[:, g * gsz:(g + 1) * gsz]
        outs.append(yg * lax.rsqrt(jnp.mean(yg * yg, axis=-1, keepdims=True) + EPS))
    y_ref[...] = (jnp.concatenate(outs, axis=1) * nw_ref[...]).astype(BF)

    @pl.when(c == pl.num_programs(1) - 1)
    def _():
        hout_ref[...] = hst[...]


def _ssd_call(batch, length, proj, dt, h0, bufx, bufbc, cwx, cbx, cwbc, cbbc, dtb, alog, dexp, nw):
    q = CHUNK
    nc = length // q
    col = lambda j: pl.BlockSpec((q, 1024), lambda b, c: (b * nc + c, j))
    npairs = SSD_HEADS // 2
    st_spec = pl.BlockSpec((None, npairs, LANES, SSD_STATE), lambda b, c: (b, 0, 0, 0))
    buf_spec = pl.BlockSpec((None, 8, 1024), lambda b, c: (b, 0, 0))
    return pl.pallas_call(
        _ssd_body,
        grid=(batch, nc),
        in_specs=[col(0), col(1), col(2), pl.BlockSpec((q, LANES), lambda b, c: (b * nc + c, 0)),
                  st_spec, buf_spec, buf_spec,
                  _full_spec((CONV_WIDTH, 1024)), _full_spec((1, 1024)), _full_spec((CONV_WIDTH, 1024)),
                  _full_spec((1, 1024)), _full_spec((1, LANES)), _full_spec((1, LANES)),
                  _full_spec((1, 1024)), _full_spec((1, 1024))],
        out_specs=[pl.BlockSpec((q, 1024), lambda b, c: (b * nc + c, 0)), st_spec],
        out_shape=[jax.ShapeDtypeStruct((batch * length, 1024), BF),
                   jax.ShapeDtypeStruct((batch, npairs, LANES, SSD_STATE), F32)],
        scratch_shapes=[pltpu.VMEM((q + 8, 1024), F32), pltpu.VMEM((q + 8, 1024), F32),
                        pltpu.VMEM((npairs, LANES, SSD_STATE), F32)],
        compiler_params=_cparams("arbitrary", "arbitrary"),
        name="ssd",
    )(proj, proj, proj, dt, h0, bufx, bufbc, cwx, cbx, cwbc, cbbc, dtb, alog, dexp, nw)


def _lru_body(gate_ref, xl_ref, h0_ref, buf_ref, cw_ref, cb_ref, wa_ref, ba_ref, wx_ref, bx_ref, lam_ref,
              y_ref, hout_ref, ext, hprev, *, last_chunk, last_row):
    t = CHUNK
    c = pl.program_id(1)
    first = c == 0

    @pl.when(first)
    def _():
        hprev[...] = h0_ref[...]

    x = _causal_conv(ext, xl_ref[...], buf_ref, cw_ref, cb_ref, first)
    x16 = x.astype(BF)
    nb = LRU_WIDTH // LANES
    ra = jnp.concatenate([_dot(x16[:, p * LANES:(p + 1) * LANES], wa_ref[p]) for p in range(nb)], axis=1)
    rx = jnp.concatenate([_dot(x16[:, p * LANES:(p + 1) * LANES], wx_ref[p]) for p in range(nb)], axis=1)
    r = jax.nn.sigmoid(ra + ba_ref[...])
    i = jax.nn.sigmoid(rx + bx_ref[...])
    log_a = -LRU_C * r * _softplus(-lam_ref[...])
    a = jnp.exp(log_a)
    u = jnp.sqrt(1.0 - jnp.exp(2.0 * log_a)) * (i * x)
    row = lax.broadcasted_iota(jnp.int32, (t, LRU_WIDTH), 0)
    sh = 1
    while sh < t:
        m = row >= sh
        u = jnp.where(m, a * pltpu.roll(u, sh, 0) + u, u)
        a = jnp.where(m, a * pltpu.roll(a, sh, 0), a)
        sh *= 2
    hs = a * hprev[...] + u
    hprev[...] = hs[t - 1:t, :]
    y_ref[...] = (hs * jax.nn.gelu(gate_ref[...])).astype(BF)

    @pl.when(c == last_chunk)
    def _():
        hout_ref[...] = hs[last_row:last_row + 1, :]


def _lru_call(batch, length, n_real, proj, h0, buf, cw, cb, wa, ba, wx, bx, lam):
    t = CHUNK
    nc = length // t
    col = lambda j: pl.BlockSpec((t, 1024), lambda b, c: (b * nc + c, j))
    vec = pl.BlockSpec((None, 1, LRU_WIDTH), lambda b, c: (b, 0, 0))
    nb = LRU_WIDTH // LANES
    last = n_real - 1
    return pl.pallas_call(
        functools.partial(_lru_body, last_chunk=last // t, last_row=last % t),
        grid=(batch, nc),
        in_specs=[col(3), col(4), vec, pl.BlockSpec((None, 8, 1024), lambda b, c: (b, 0, 0)),
                  _full_spec((CONV_WIDTH, 1024)), _full_spec((1, 1024)),
                  _full_spec((nb, LANES, LANES)), _full_spec((1, 1024)),
                  _full_spec((nb, LANES, LANES)), _full_spec((1, 1024)), _full_spec((1, 1024))],
        out_specs=[pl.BlockSpec((t, 1024), lambda b, c: (b * nc + c, 0)), vec],
        out_shape=[jax.ShapeDtypeStruct((batch * length, 1024), BF),
                   jax.ShapeDtypeStruct((batch, 1, LRU_WIDTH), F32)],
        scratch_shapes=[pltpu.VMEM((t + 8, 1024), F32), pltpu.VMEM((1, LRU_WIDTH), F32)],
        compiler_params=_cparams("arbitrary", "arbitrary"),
        name="rg_lru",
    )(proj, proj, h0, buf, cw, cb, wa, ba, wx, bx, lam)


def _rope_tables(pos):
    half = HEAD_DIM // 2
    inv = ROPE_THETA ** (-jnp.arange(half, dtype=F32) / half)
    ang = pos.astype(F32)[:, None] * inv[None, :]
    cos = jnp.cos(ang)
    sin = jnp.sin(ang)
    cos = jnp.concatenate([cos, cos, cos, cos], axis=1)
    sin = jnp.concatenate([-sin, sin, -sin, sin], axis=1)
    return cos, sin


def _block_diag_pairs(w):
    nb = w.shape[0] // 2
    w2 = w.reshape(nb, 2, HEAD_DIM, HEAD_DIM)
    z = jnp.zeros((nb, HEAD_DIM, HEAD_DIM), w.dtype)
    top = jnp.concatenate([w2[:, 0], z], axis=2)
    bot = jnp.concatenate([z, w2[:, 1]], axis=2)
    return jnp.concatenate([top, bot], axis=1)


def _buf8(buf):
    return jnp.pad(buf, ((0, 0), (8 - (CONV_WIDTH - 1), 0), (0, 0)))


def _pad128(v):
    return jnp.pad(v, (0, LANES - v.shape[0])).reshape(1, LANES)


def kernel(x_prompt, x_sample, c_prompt, c_sample, cache_sb_k, cache_sb_v, cache_diff_k, cache_diff_v, state_ssd, state_ssd_conv, state_lru, state_lru_conv, page_table, w_ada, b_ada, norm_mix, norm_ffn, attn_w_in, attn_w_out, diff_lam, diff_subln, rec_w_in, ssd_conv_w, ssd_conv_b, ssd_dt_bias, ssd_a_log, ssd_d, ssd_norm, lru_conv_w, lru_conv_b, lru_w_a, lru_b_a, lru_w_x, lru_b_x, lru_lam, rec_w_out, ffn_w_gate, ffn_w_up, ffn_w_down, w_ada_final, b_ada_final, norm_final):
    bp, lp, d = x_prompt.shape
    bs, ls, _ = x_sample.shape
    n_pages = page_table.shape[1]
    n_phys = cache_sb_k.shape[1]
    na = attn_w_in.shape[0]
    nr = rec_w_in.shape[0]
    gp = _Group(bp, lp, per_row=False)
    gs = _Group(bs, ls, per_row=True)

    nb_all = bp + bs
    mp = -(-nb_all // 8) * 8
    c_all = jnp.pad(jnp.concatenate([c_prompt, c_sample], axis=0), ((0, mp - nb_all), (0, 0)))
    mods = _ada_call(c_all, w_ada, b_ada.reshape(DEPTH, 1, 6 * d))
    modf = _ada_call(c_all, w_ada_final[None], b_ada_final.reshape(1, 1, 2 * d))[0]

    def group_mods(m2d, n):
        parts = jnp.split(m2d, n, axis=-1)
        p_parts = [v[:bp].reshape(bp, 1, d) for v in parts]
        s_parts = [jnp.repeat(v[bp:bp + bs], ls, axis=0) for v in parts]
        return p_parts, s_parts

    w_attn_in = attn_w_in.astype(BF)
    w_attn_out = attn_w_out.astype(BF)
    w_rec_out = rec_w_out.astype(BF)
    wg = ffn_w_gate.astype(BF)
    wu = ffn_w_up.astype(BF)
    wd = ffn_w_down.astype(BF)
    s0, s1, s2 = 1024, 3072, 3088
    w_rec_main = jnp.concatenate([rec_w_in[:, :, :s1], rec_w_in[:, :, s2:]], axis=2).astype(BF)
    w_rec_dt = jnp.pad(rec_w_in[:, :, s1:s2], ((0, 0), (0, 0), (0, LANES - SSD_HEADS))).astype(BF)
    wa_bd = jax.vmap(_block_diag_pairs)(lru_w_a).astype(BF)
    wx_bd = jax.vmap(_block_diag_pairs)(lru_w_x).astype(BF)

    cos_p, sin_p = _rope_tables(jnp.arange(lp))
    pos_s = n_pages * PAGE + jnp.arange(ls)
    cos_s, sin_s = _rope_tables(jnp.tile(pos_s, bs))
    pt_flat = page_table.reshape(-1).astype(jnp.int32)

    kt_shape = (na * n_phys, N_CLS, HEAD_DIM, PAGE)
    kc_sb = cache_sb_k.transpose(0, 1, 3, 4, 2).reshape(kt_shape)
    vc_sb = cache_sb_v.transpose(0, 1, 3, 4, 2).reshape(kt_shape)
    kc_d = cache_diff_k.transpose(0, 1, 3, 4, 5, 2).reshape(kt_shape)
    vc_d = cache_diff_v.reshape(na * n_phys, PAGE * DIFF_HEADS, LANES)

    def q_rows(q):
        q4 = q.reshape(bs, ls, N_CLS, HEAD_DIM).transpose(0, 2, 1, 3)
        return jnp.pad(q4, ((0, 0), (0, 0), (0, N_QPAD - ls), (0, 0))).reshape(bs, Q_ROWS, HEAD_DIM)

    def new_page_t(k):
        k4 = k.reshape(bs, ls, N_CLS, HEAD_DIM).transpose(0, 2, 3, 1)
        return jnp.pad(k4, ((0, 0), (0, 0), (0, 0), (0, PAGE - ls)))

    def from_rows(o, n_cls):
        o4 = o.reshape(bs, n_cls, N_QPAD, -1).transpose(0, 2, 1, 3)[:, :ls]
        return o4.reshape(bs * ls, SB_WIDTH).astype(BF)

    xp = x_prompt.reshape(bp * lp, d)
    xs = x_sample.reshape(bs * ls, d)
    out_p = {k: [] for k in ("sbk", "sbv", "dk", "dv", "ssd", "ssdc", "lru", "lruc")}
    out_s = {k: [] for k in ("sbk", "sbv", "dk", "dv", "ssd", "ssdc", "lru", "lruc")}
    lpad = CHUNK

    for li in range(DEPTH):
        (sh1p, sc1p, g1p, sh2p, sc2p, g2p), (sh1s, sc1s, g1s, sh2s, sc2s, g2s) = group_mods(mods[li], 6)
        gmix = norm_mix[li].reshape(1, d)
        if li % 2 == 0:
            ai = li // 2
            lam_init = 0.8 - 0.6 * math.exp(-0.3 * li)
            gain = diff_subln[ai].reshape(1, LANES)
            qsb, ksb, ksbh, vsb, vsbh, qd, kd, kdh, vd, vdh = _attn_in_call(
                gp, xp, gmix, sh1p, sc1p, w_attn_in[ai], cos_p, sin_p)
            o_sb = _sb_prompt_call(bp, lp, qsb, ksbh, vsbh)
            o_d = _diff_prompt_call(bp, lp, qd, kdh, vdh, diff_lam[ai], gain, lam_init)
            xp = _out_proj_call(gp, o_sb, o_d, w_attn_out[ai], xp, g1p)
            out_p["sbk"].append(ksb.reshape(bp, lp, SB_HEADS, HEAD_DIM))
            out_p["sbv"].append(vsb.reshape(bp, lp, SB_HEADS, HEAD_DIM))
            out_p["dk"].append(kd.reshape(bp, lp, DIFF_HEADS, 2, HEAD_DIM))
            out_p["dv"].append(vd.reshape(bp, lp, DIFF_HEADS, 2 * HEAD_DIM))
            qsb, ksb, _, vsb, _, qd, kd, _, vd, _ = _attn_in_call(
                gs, xs, gmix, sh1s, sc1s, w_attn_in[ai], cos_s, sin_s)

            base = ai * n_phys
            o = _sb_sample_call(pt_flat, n_pages, base, q_rows(qsb), kc_sb, vc_sb,
                                new_page_t(ksb), new_page_t(vsb))
            o_sb = from_rows(o, N_CLS)
            vd_page = jnp.pad(vd.reshape(bs, ls * DIFF_HEADS, LANES),
                              ((0, 0), (0, (PAGE - ls) * DIFF_HEADS), (0, 0)))
            o = _diff_sample_call(pt_flat, n_pages, base, q_rows(qd), kc_d, vc_d,
                                  new_page_t(kd), vd_page, diff_lam[ai], gain, lam_init)
            o_d = from_rows(o, DIFF_HEADS)
            xs = _out_proj_call(gs, o_sb, o_d, w_attn_out[ai], xs, g1s)
            out_s["sbk"].append(ksb.reshape(bs, ls, SB_HEADS, HEAD_DIM))
            out_s["sbv"].append(vsb.reshape(bs, ls, SB_HEADS, HEAD_DIM))
            out_s["dk"].append(kd.reshape(bs, ls, DIFF_HEADS, 2, HEAD_DIM))
            out_s["dv"].append(vd.reshape(bs, ls, DIFF_HEADS, 2 * HEAD_DIM))
        else:
            ri = li // 2
            cw = ssd_conv_w[ri]
            cb = ssd_conv_b[ri].reshape(1, -1)
            ssd_args = (cw[:, :1024], cb[:, :1024], cw[:, 1024:], cb[:, 1024:],
                        _pad128(ssd_dt_bias[ri]), _pad128(ssd_a_log[ri]),
                        jnp.repeat(ssd_d[ri], HEAD_DIM).reshape(1, SSD_WIDTH), ssd_norm[ri].reshape(1, SSD_WIDTH))
            lru_args = (lru_conv_w[ri], lru_conv_b[ri].reshape(1, -1), wa_bd[ri], lru_b_a[ri].reshape(1, -1),
                        wx_bd[ri], lru_b_x[ri].reshape(1, -1), lru_lam[ri].reshape(1, -1))
            npairs = SSD_HEADS // 2
            proj, dt = _rec_in_call(gp, xp, gmix, sh1p, sc1p, w_rec_main[ri], w_rec_dt[ri])
            zeros8 = jnp.zeros((bp, 8, 1024), F32)
            y_ssd, h_ssd = _ssd_call(bp, lp, proj, dt, jnp.zeros((bp, npairs, LANES, SSD_STATE), F32),
                                     zeros8, zeros8, *ssd_args)
            y_lru, h_lru = _lru_call(bp, lp, lp, proj, jnp.zeros((bp, 1, LRU_WIDTH), F32), zeros8, *lru_args)
            xp = _out_proj_call(gp, y_ssd, y_lru, w_rec_out[ri], xp, g1p)
            p3 = proj.reshape(bp, lp, REC_MAIN)
            out_p["ssd"].append(h_ssd.reshape(bp, SSD_HEADS, HEAD_DIM, SSD_STATE))
            out_p["ssdc"].append(p3[:, lp - (CONV_WIDTH - 1):, s0:s1])
            out_p["lru"].append(h_lru.reshape(bp, LRU_WIDTH))
            out_p["lruc"].append(p3[:, lp - (CONV_WIDTH - 1):, 4096:])
            proj, dt = _rec_in_call(gs, xs, gmix, sh1s, sc1s, w_rec_main[ri], w_rec_dt[ri])
            p3 = proj.reshape(bs, ls, REC_MAIN)
            proj_pad = jnp.pad(p3, ((0, 0), (0, lpad - ls), (0, 0))).reshape(bs * lpad, REC_MAIN)
            dt_pad = jnp.pad(dt.reshape(bs, ls, LANES), ((0, 0), (0, lpad - ls), (0, 0)),
                             constant_values=-jnp.inf).reshape(bs * lpad, LANES)
            sbuf = _buf8(state_ssd_conv[ri])
            y_ssd, h_ssd = _ssd_call(bs, lpad, proj_pad, dt_pad,
                                     state_ssd[ri].reshape(bs, npairs, LANES, SSD_STATE),
                                     sbuf[:, :, :1024], sbuf[:, :, 1024:], *ssd_args)
            y_lru, h_lru = _lru_call(bs, lpad, ls, proj_pad, state_lru[ri].reshape(bs, 1, LRU_WIDTH),
                                     _buf8(state_lru_conv[ri]), *lru_args)
            y_ssd = y_ssd.reshape(bs, lpad, SSD_WIDTH)[:, :ls].reshape(bs * ls, SSD_WIDTH)
            y_lru = y_lru.reshape(bs, lpad, LRU_WIDTH)[:, :ls].reshape(bs * ls, LRU_WIDTH)
            xs = _out_proj_call(gs, y_ssd, y_lru, w_rec_out[ri], xs, g1s)
            out_s["ssd"].append(h_ssd.reshape(bs, SSD_HEADS, HEAD_DIM, SSD_STATE))
            out_s["ssdc"].append(p3[:, ls - (CONV_WIDTH - 1):, s0:s1])
            out_s["lru"].append(h_lru.reshape(bs, LRU_WIDTH))
            out_s["lruc"].append(p3[:, ls - (CONV_WIDTH - 1):, 4096:])
        gffn = norm_ffn[li].reshape(1, d)
        xp = _ffn_call(gp, xp, gffn, sh2p, sc2p, g2p, wg[li], wu[li], wd[li])
        xs = _ffn_call(gs, xs, gffn, sh2s, sc2s, g2s, wg[li], wu[li], wd[li])

    (shp, scp), (shs, scs) = group_mods(modf, 2)
    gfin = norm_final.reshape(1, d)
    y_prompt = _final_call(gp, xp, gfin, shp, scp).reshape(bp, lp, d)
    y_sample = _final_call(gs, xs, gfin, shs, scs).reshape(bs, ls, d)

    order = ("sbk", "sbv", "dk", "dv", "ssd", "ssdc", "lru", "lruc")
    return (y_prompt, y_sample) + tuple(jnp.stack(out_p[k]) for k in order) + tuple(jnp.stack(out_s[k]) for k in order)
```

```python
import functools
import math

import jax
import jax.numpy as jnp
from jax import lax
from jax.experimental import pallas as pl
from jax.experimental.pallas import tpu as pltpu

BF = jnp.bfloat16
F32 = jnp.float32

D_MODEL = 1024
DEPTH = 4
PAGE = 128
HEAD_DIM = 64
SB_HEADS = 8
SB_WIDTH = SB_HEADS * HEAD_DIM
DIFF_HEADS = 4
ATTN_IN = 6 * SB_WIDTH
ROPE_THETA = 10000.0
NEG_INF = -1e30
SSD_HEADS = 16
SSD_WIDTH = 1024
SSD_GROUPS = 4
SSD_STATE = 128
CONV_WIDTH = 4
LRU_WIDTH = 1024
LRU_BLOCKS = 16
LRU_C = 8.0
FFN_HIDDEN = 2816
EPS = 1e-6

LANES = 128
ROW_TILE = 512
ATT_TQ = 256
ATT_TK = 512
CHUNK = 128
FFN_TILE = 1408
VMEM_LIMIT = 52 * 1024 * 1024


def _cparams(*sem):
    return pltpu.CompilerParams(dimension_semantics=sem, vmem_limit_bytes=VMEM_LIMIT)


def _dot(a, b):
    return jnp.dot(a, b, preferred_element_type=F32)


def _dot_nt(a, b):
    return lax.dot_general(a, b, (((1,), (1,)), ((), ())), preferred_element_type=F32)


def _softplus(x):
    return jnp.maximum(x, 0.0) + jnp.log1p(jnp.exp(-jnp.abs(x)))


def _log_sigmoid(z):
    return jnp.minimum(z, 0.0) - jnp.log(1.0 + jnp.exp(-jnp.abs(z)))


def _modulate(x, g, sh, sc):
    y = x * lax.rsqrt(jnp.mean(x * x, axis=-1, keepdims=True) + EPS) * g
    return y * (1.0 + sc) + sh


def _split3(x):
    a = x.astype(BF)
    r = x - a.astype(F32)
    b = r.astype(BF)
    c = (r - b.astype(F32)).astype(BF)
    return a, b, c


def _split2(x):
    a = x.astype(BF)
    b = (x - a.astype(F32)).astype(BF)
    return a, b


def _ada_body(c_ref, w_ref, b_ref, o_ref):
    h = jax.nn.silu(c_ref[...]).astype(BF)
    o_ref[...] = _dot(h, w_ref[...].astype(BF)) + b_ref[...]


def _ada_call(c_all, w, b):
    nl, d, n = w.shape
    mp = c_all.shape[0]
    tn = 1024
    return pl.pallas_call(
        _ada_body,
        grid=(nl, n // tn),
        in_specs=[pl.BlockSpec((mp, d), lambda l, j: (0, 0)),
                  pl.BlockSpec((None, d, tn), lambda l, j: (l, 0, j)),
                  pl.BlockSpec((None, 1, tn), lambda l, j: (l, 0, j))],
        out_specs=pl.BlockSpec((None, mp, tn), lambda l, j: (l, 0, j)),
        out_shape=jax.ShapeDtypeStruct((nl, mp, n), F32),
        compiler_params=_cparams("arbitrary", "arbitrary"),
        name="ada",
    )(c_all, w, b)


class _Group:
    def __init__(self, batch, length, per_row):
        self.batch = batch
        self.length = length
        self.rows = batch * length
        self.per_row = per_row
        self.tm = min(ROW_TILE, self.rows)
        self.blocks_per_seq = 1 if per_row else length // self.tm

    def mod_spec(self, d=D_MODEL):
        if self.per_row:
            return pl.BlockSpec((self.tm, d), lambda i, *_: (i, 0))
        bps = self.blocks_per_seq
        return pl.BlockSpec((None, 1, d), lambda i, *_: (i // bps, 0, 0))

    def pos_spec(self):
        if self.per_row:
            return pl.BlockSpec((self.tm, LANES), lambda i, *_: (i, 0))
        bps = self.blocks_per_seq
        return pl.BlockSpec((self.tm, LANES), lambda i, *_: (i % bps, 0))


def _row_spec(tm, n):
    return pl.BlockSpec((tm, n), lambda i, *_: (i, 0))


def _full_spec(shape):
    nd = len(shape)
    return pl.BlockSpec(shape, lambda *_: (0,) * nd)


def _rope(x, cos, sin):
    n = x.shape[1] // LANES
    c = jnp.concatenate([cos] * n, axis=1)
    s = jnp.concatenate([sin] * n, axis=1)
    lane = lax.broadcasted_iota(jnp.int32, x.shape, 1)
    first = (lane % HEAD_DIM) < (HEAD_DIM // 2)
    w = x.shape[1]
    partner = jnp.where(first, pltpu.roll(x, w - HEAD_DIM // 2, 1), pltpu.roll(x, HEAD_DIM // 2, 1))
    return x * c + partner * s


def _attn_in_body(x_ref, g_ref, sh_ref, sc_ref, w_ref, cos_ref, sin_ref,
                  qsb_ref, ksb_ref, ksbh_ref, vsb_ref, vsbh_ref, qd_ref, kd_ref, kdh_ref, vd_ref, vdh_ref):
    h = _modulate(x_ref[...], g_ref[...], sh_ref[...], sc_ref[...]).astype(BF)
    proj = _dot(h, w_ref[...])
    w = SB_WIDTH
    scale = HEAD_DIM ** -0.5
    qsb_ref[...] = (proj[:, 0:w] * scale).astype(BF)
    k = proj[:, w:2 * w]
    ksb_ref[...] = k
    ksbh_ref[...] = k.astype(BF)
    v = proj[:, 2 * w:3 * w]
    vsb_ref[...] = v
    vsbh_ref[...] = v.astype(BF)
    cos = cos_ref[...]
    sin = sin_ref[...]
    qd_ref[...] = (_rope(proj[:, 3 * w:4 * w], cos, sin) * scale).astype(BF)
    kd = _rope(proj[:, 4 * w:5 * w], cos, sin)
    kd_ref[...] = kd
    kdh_ref[...] = kd.astype(BF)
    vd = proj[:, 5 * w:6 * w]
    vd_ref[...] = vd
    vdh_ref[...] = vd.astype(BF)


def _attn_in_call(grp, x, g, sh, sc, w, cos, sin):
    m, tm = grp.rows, grp.tm
    o32 = jax.ShapeDtypeStruct((m, SB_WIDTH), F32)
    o16 = jax.ShapeDtypeStruct((m, SB_WIDTH), BF)
    spec = _row_spec(tm, SB_WIDTH)
    return pl.pallas_call(
        _attn_in_body,
        grid=(m // tm,),
        in_specs=[_row_spec(tm, D_MODEL), _full_spec((1, D_MODEL)), grp.mod_spec(), grp.mod_spec(),
                  _full_spec((D_MODEL, ATTN_IN)), grp.pos_spec(), grp.pos_spec()],
        out_specs=[spec] * 10,
        out_shape=[o16, o32, o16, o32, o16, o16, o32, o16, o32, o16],
        compiler_params=_cparams("arbitrary"),
        name="attn_in",
    )(x, g, sh, sc, w, cos, sin)


def _sb_prompt_body(q_ref, k_ref, v_ref, o_ref):
    tq, tk = ATT_TQ, ATT_TK
    qi = pl.program_id(2)
    q = q_ref[...]
    lane = lax.broadcasted_iota(jnp.int32, (tq, LANES), 1)
    zero = jnp.zeros_like(q)
    qs = (jnp.where(lane < HEAD_DIM, q, zero), jnp.where(lane >= HEAD_DIM, q, zero))
    krow = lax.broadcasted_iota(jnp.int32, (tk, tk), 0)
    kcol = lax.broadcasted_iota(jnp.int32, (tk, tk), 1)
    later = (krow > kcol).astype(BF)
    row = lax.broadcasted_iota(jnp.int32, (tq, tk), 0) + qi * tq
    col = lax.broadcasted_iota(jnp.int32, (tq, tk), 1)

    def tile(kb, carry, masked):
        start = pl.multiple_of(kb * tk, tk)
        k = k_ref[pl.ds(start, tk), :]
        v = v_ref[pl.ds(start, tk), :]
        out = []
        for qh, (run, acc) in zip(qs, carry):
            z = _dot_nt(qh, k)
            lb = _log_sigmoid(z)
            lk = lb - z
            if masked:
                valid = col + kb * tk < row
                lk = jnp.where(valid, lk, 0.0)
            hi, lo = _split2(lk)
            rest = _dot(hi, later) + _dot(lo, later) + run
            wgt = jnp.exp(lb + rest)
            if masked:
                wgt = jnp.where(valid, wgt, 0.0)
            acc = acc + _dot(wgt.astype(BF), v)
            run = run + jnp.sum(lk, axis=1, keepdims=True)
            out.append((run, acc))
        return tuple(out)

    init = tuple((jnp.zeros((tq, 1), F32), jnp.zeros((tq, LANES), F32)) for _ in range(2))
    kd = (qi * tq) // tk
    carry = tile(kd, init, True)
    carry = lax.fori_loop(0, kd, lambda i, c: tile(kd - 1 - i, c, False), carry)
    o_ref[...] = jnp.where(lane < HEAD_DIM, carry[0][1], carry[1][1]).astype(BF)


def _prompt_attn_specs(batch, length):
    tq = ATT_TQ
    assert ATT_TK % tq == 0 and length % ATT_TK == 0
    nq = length // tq
    q_spec = pl.BlockSpec((tq, LANES), lambda b, p, i: (b * nq + i, p))
    kv_spec = pl.BlockSpec((length, LANES), lambda b, p, i: (b, p))
    return nq, q_spec, kv_spec


def _sb_prompt_call(batch, length, q, k, v):
    nq, q_spec, kv_spec = _prompt_attn_specs(batch, length)
    return pl.pallas_call(
        _sb_prompt_body,
        grid=(batch, SB_WIDTH // LANES, nq),
        in_specs=[q_spec, kv_spec, kv_spec],
        out_specs=q_spec,
        out_shape=jax.ShapeDtypeStruct(q.shape, BF),
        compiler_params=_cparams("arbitrary", "arbitrary", "arbitrary"),
        name="sb_prompt",
    )(q, k, v)


def _diff_lambda(lam_ref, lam_init):
    lv = lam_ref[...]
    s1 = jnp.sum(lv[0:1] * lv[1:2], axis=1, keepdims=True)
    s2 = jnp.sum(lv[2:3] * lv[3:4], axis=1, keepdims=True)
    return jnp.exp(s1) - jnp.exp(s2) + lam_init


def _diff_finish(o0, o1, lam, gain, lam_init):
    o = o0 - lam * o1
    y = o * lax.rsqrt(jnp.mean(o * o, axis=-1, keepdims=True) + EPS) * gain
    return y * (1.0 - lam_init)


def _diff_prompt_body(q_ref, k_ref, v_ref, lam_ref, gain_ref, o_ref, *, lam_init):
    tq, tk = ATT_TQ, ATT_TK
    qi = pl.program_id(2)
    q = q_ref[...]
    lane = lax.broadcasted_iota(jnp.int32, (tq, LANES), 1)
    zero = jnp.zeros_like(q)
    qs = (jnp.where(lane < HEAD_DIM, q, zero), jnp.where(lane >= HEAD_DIM, q, zero))
    row = lax.broadcasted_iota(jnp.int32, (tq, tk), 0) + qi * tq
    col = lax.broadcasted_iota(jnp.int32, (tq, tk), 1)

    def tile(kb, carry, masked):
        start = pl.multiple_of(kb * tk, tk)
        k = k_ref[pl.ds(start, tk), :]
        v = v_ref[pl.ds(start, tk), :]
        out = []
        for qh, (mx, den, acc) in zip(qs, carry):
            z = _dot_nt(qh, k)
            if masked:
                z = jnp.where(col + kb * tk <= row, z, NEG_INF)
            mx_new = jnp.maximum(mx, jnp.max(z, axis=1, keepdims=True))
            alpha = jnp.exp(mx - mx_new)
            p = jnp.exp(z - mx_new)
            den = alpha * den + jnp.sum(p, axis=1, keepdims=True)
            acc = alpha * acc + _dot(p.astype(BF), v)
            out.append((mx_new, den, acc))
        return tuple(out)

    init = tuple((jnp.full((tq, 1), NEG_INF, F32), jnp.zeros((tq, 1), F32), jnp.zeros((tq, LANES), F32))
                 for _ in range(2))
    kd = (qi * tq) // tk
    carry = tile(kd, init, True)
    carry = lax.fori_loop(0, kd, lambda i, c: tile(kd - 1 - i, c, False), carry)
    lam = _diff_lambda(lam_ref, lam_init)
    o0 = carry[0][2] / carry[0][1]
    o1 = carry[1][2] / carry[1][1]
    o_ref[...] = _diff_finish(o0, o1, lam, gain_ref[...], lam_init).astype(BF)


def _diff_prompt_call(batch, length, q, k, v, lam, gain, lam_init):
    nq, q_spec, kv_spec = _prompt_attn_specs(batch, length)
    return pl.pallas_call(
        functools.partial(_diff_prompt_body, lam_init=lam_init),
        grid=(batch, DIFF_HEADS, nq),
        in_specs=[q_spec, kv_spec, kv_spec, _full_spec((4, HEAD_DIM)), _full_spec((1, LANES))],
        out_specs=q_spec,
        out_shape=jax.ShapeDtypeStruct(q.shape, BF),
        compiler_params=_cparams("arbitrary", "arbitrary", "arbitrary"),
        name="diff_prompt",
    )(q, k, v, lam, gain)


N_CLS = 8
N_QPAD = 8
PAGES_PER_STEP = 8
Q_ROWS = N_CLS * N_QPAD


def _page_specs(n_pages, base, block):
    def spec(slot):
        def idx(b, j, pt_ref):
            page = n_pages - jnp.maximum(j, 1) * PAGES_PER_STEP + slot
            return (base + pt_ref[b * n_pages + page],) + (0,) * (len(block) - 1)
        return pl.BlockSpec(block, idx)
    return [spec(s) for s in range(PAGES_PER_STEP)]


def _class_scores(q, k_ref):
    return jnp.concatenate(
        [_dot(q[c * N_QPAD:(c + 1) * N_QPAD], k_ref[c].astype(BF)) for c in range(N_CLS)], axis=0)


def _sb_sample_body(pt_ref, q_ref, *refs, n_steps):
    np_ = PAGES_PER_STEP
    kc, vc = refs[0:np_], refs[np_:2 * np_]
    kn_ref, vn_ref, o_ref, run_s, acc_s = refs[2 * np_:]
    j = pl.program_id(1)
    q = q_ref[...]
    row = lax.broadcasted_iota(jnp.int32, (PAGE, PAGE), 0)
    col = lax.broadcasted_iota(jnp.int32, (PAGE, PAGE), 1)
    later = (row > col).astype(BF)
    tok = lax.broadcasted_iota(jnp.int32, (Q_ROWS, PAGE), 1)
    qpos = lax.broadcasted_iota(jnp.int32, (Q_ROWS, PAGE), 0) % N_QPAD

    def page(k_ref, v_ref, new):
        z = _class_scores(q, k_ref)
        lb = _log_sigmoid(z)
        lk = lb - z
        if new:
            valid = tok < qpos
            lk = jnp.where(valid, lk, 0.0)
        hi, lo = _split2(lk)
        wgt = jnp.exp(lb + _dot(hi, later) + _dot(lo, later))
        if new:
            wgt = jnp.where(valid, wgt, 0.0)
        w16 = wgt.astype(BF)
        pv = jnp.concatenate(
            [_dot_nt(w16[c * N_QPAD:(c + 1) * N_QPAD], v_ref[c].astype(BF)) for c in range(N_CLS)], axis=0)
        return pv, jnp.sum(lk, axis=1, keepdims=True)

    def fold(parts):
        run, acc = run_s[...], acc_s[...]
        for pv, tot in parts:
            acc = acc + jnp.exp(run) * pv
            run = run + tot
        run_s[...] = run
        acc_s[...] = acc

    @pl.when(j == 0)
    def _():
        run_s[...] = jnp.zeros_like(run_s)
        acc_s[...] = jnp.zeros_like(acc_s)
        fold([page(kn_ref, vn_ref, True)])

    @pl.when(j > 0)
    def _():
        fold([page(kc[slot], vc[slot], False) for slot in reversed(range(np_))])

    @pl.when(j == n_steps - 1)
    def _():
        o_ref[...] = acc_s[...]


def _sb_sample_call(pt_flat, n_pages, base, q, kc, vc, kn, vn):
    nb = q.shape[0]
    n_steps = n_pages // PAGES_PER_STEP + 1
    block = (None, N_CLS, HEAD_DIM, PAGE)
    per_b = lambda *s: pl.BlockSpec((None,) + s, lambda b, j, pt: (b,) + (0,) * len(s))
    k_specs = _page_specs(n_pages, base, block)
    return pl.pallas_call(
        functools.partial(_sb_sample_body, n_steps=n_steps),
        grid_spec=pltpu.PrefetchScalarGridSpec(
            num_scalar_prefetch=1,
            grid=(nb, n_steps),
            in_specs=[per_b(Q_ROWS, HEAD_DIM)] + k_specs + k_specs
                     + [per_b(N_CLS, HEAD_DIM, PAGE), per_b(N_CLS, HEAD_DIM, PAGE)],
            out_specs=per_b(Q_ROWS, HEAD_DIM),
            scratch_shapes=[pltpu.VMEM((Q_ROWS, 1), F32), pltpu.VMEM((Q_ROWS, HEAD_DIM), F32)]),
        out_shape=jax.ShapeDtypeStruct((nb, Q_ROWS, HEAD_DIM), F32),
        compiler_params=_cparams("arbitrary", "arbitrary"),
        name="sb_sample",
    )(pt_flat, q, *([kc] * PAGES_PER_STEP), *([vc] * PAGES_PER_STEP), kn, vn)


def _diff_sample_body(pt_ref, q_ref, *refs, n_steps, lam_init):
    np_ = PAGES_PER_STEP
    kc, vc = refs[0:np_], refs[np_:2 * np_]
    kn_ref, vn_ref, lam_ref, gain_ref, o_ref, mx_s, den_s, acc_s = refs[2 * np_:]
    j = pl.program_id(1)
    q = q_ref[...]
    tok = lax.broadcasted_iota(jnp.int32, (Q_ROWS, PAGE), 1)
    qpos = lax.broadcasted_iota(jnp.int32, (Q_ROWS, PAGE), 0) % N_QPAD
    rows_per_head = 2 * N_QPAD

    def page(k_ref, v_ref, new):
        z = _class_scores(q, k_ref)
        if new:
            z = jnp.where(tok <= qpos, z, NEG_INF)
        mloc = jnp.max(z, axis=1, keepdims=True)
        p = jnp.exp(z - mloc)
        p16 = p.astype(BF)
        pv = jnp.concatenate(
            [_dot(p16[h * rows_per_head:(h + 1) * rows_per_head],
                  v_ref[pl.ds(h, PAGE, stride=DIFF_HEADS), :].astype(BF)) for h in range(DIFF_HEADS)], axis=0)
        return mloc, jnp.sum(p, axis=1, keepdims=True), pv

    def fold(parts):
        mx, den, acc = mx_s[...], den_s[...], acc_s[...]
        for mloc, lloc, pv in parts:
            mx_new = jnp.maximum(mx, mloc)
            a = jnp.exp(mx - mx_new)
            b = jnp.exp(mloc - mx_new)
            den = a * den + b * lloc
            acc = a * acc + b * pv
            mx = mx_new
        mx_s[...] = mx
        den_s[...] = den
        acc_s[...] = acc

    @pl.when(j == 0)
    def _():
        mx_s[...] = jnp.full_like(mx_s, NEG_INF)
        den_s[...] = jnp.zeros_like(den_s)
        acc_s[...] = jnp.zeros_like(acc_s)
        fold([page(kn_ref, vn_ref, True)])

    @pl.when(j > 0)
    def _():
        fold([page(kc[slot], vc[slot], False) for slot in reversed(range(np_))])

    @pl.when(j == n_steps - 1)
    def _():
        o = acc_s[...] / den_s[...]
        o0 = jnp.concatenate([o[h * rows_per_head:h * rows_per_head + N_QPAD] for h in range(DIFF_HEADS)], axis=0)
        o1 = jnp.concatenate([o[h * rows_per_head + N_QPAD:(h + 1) * rows_per_head] for h in range(DIFF_HEADS)],
                             axis=0)
        lam = _diff_lambda(lam_ref, lam_init)
        o_ref[...] = _diff_finish(o0, o1, lam, gain_ref[...], lam_init)


def _diff_sample_call(pt_flat, n_pages, base, q, kc, vc, kn, vn, lam, gain, lam_init):
    nb = q.shape[0]
    n_steps = n_pages // PAGES_PER_STEP + 1
    v_rows = PAGE * DIFF_HEADS
    per_b = lambda *s: pl.BlockSpec((None,) + s, lambda b, j, pt: (b,) + (0,) * len(s))
    const = lambda shape: pl.BlockSpec(shape, lambda b, j, pt: (0,) * len(shape))
    k_specs = _page_specs(n_pages, base, (None, N_CLS, HEAD_DIM, PAGE))
    v_specs = _page_specs(n_pages, base, (None, v_rows, LANES))
    return pl.pallas_call(
        functools.partial(_diff_sample_body, n_steps=n_steps, lam_init=lam_init),
        grid_spec=pltpu.PrefetchScalarGridSpec(
            num_scalar_prefetch=1,
            grid=(nb, n_steps),
            in_specs=[per_b(Q_ROWS, HEAD_DIM)] + k_specs + v_specs
                     + [per_b(N_CLS, HEAD_DIM, PAGE), per_b(v_rows, LANES), const((4, HEAD_DIM)), const((1, LANES))],
            out_specs=per_b(Q_ROWS // 2, LANES),
            scratch_shapes=[pltpu.VMEM((Q_ROWS, 1), F32), pltpu.VMEM((Q_ROWS, 1), F32),
                            pltpu.VMEM((Q_ROWS, LANES), F32)]),
        out_shape=jax.ShapeDtypeStruct((nb, Q_ROWS // 2, LANES), F32),
        compiler_params=_cparams("arbitrary", "arbitrary"),
        name="diff_sample",
    )(pt_flat, q, *([kc] * PAGES_PER_STEP), *([vc] * PAGES_PER_STEP), kn, vn, lam, gain)


def _out_proj_body(a_ref, b_ref, w_ref, x_ref, g_ref, o_ref):
    ka = a_ref.shape[1]
    y = _dot(a_ref[...], w_ref[0:ka, :]) + _dot(b_ref[...], w_ref[ka:, :])
    o_ref[...] = x_ref[...] + g_ref[...] * y


def _out_proj_call(grp, a, b, w, x, g):
    m, tm = grp.rows, grp.tm
    return pl.pallas_call(
        _out_proj_body,
        grid=(m // tm,),
        in_specs=[_row_spec(tm, a.shape[1]), _row_spec(tm, b.shape[1]), _full_spec(w.shape),
                  _row_spec(tm, D_MODEL), grp.mod_spec()],
        out_specs=_row_spec(tm, D_MODEL),
        out_shape=jax.ShapeDtypeStruct((m, D_MODEL), F32),
        compiler_params=_cparams("arbitrary"),
        name="out_proj",
    )(a, b, w, x, g)


def _ffn_body(x_ref, gn_ref, sh_ref, sc_ref, g2_ref, wg_ref, wu_ref, wd_ref, o_ref, h_s, acc_s):
    j = pl.program_id(1)

    @pl.when(j == 0)
    def _():
        h_s[...] = _modulate(x_ref[...], gn_ref[...], sh_ref[...], sc_ref[...]).astype(BF)
        acc_s[...] = jnp.zeros_like(acc_s)

    h = h_s[...]
    a = jax.nn.silu(_dot(h, wg_ref[...])) * _dot(h, wu_ref[...])
    acc_s[...] += _dot(a.astype(BF), wd_ref[...])

    @pl.when(j == pl.num_programs(1) - 1)
    def _():
        o_ref[...] = x_ref[...] + g2_ref[...] * acc_s[...]


def _ffn_call(grp, x, gn, sh, sc, g2, wg, wu, wd):
    m, tm = grp.rows, grp.tm
    tf = FFN_TILE
    return pl.pallas_call(
        _ffn_body,
        grid=(m // tm, FFN_HIDDEN // tf),
        in_specs=[_row_spec(tm, D_MODEL), _full_spec((1, D_MODEL)), grp.mod_spec(), grp.mod_spec(), grp.mod_spec(),
                  pl.BlockSpec((D_MODEL, tf), lambda i, j: (0, j)),
                  pl.BlockSpec((D_MODEL, tf), lambda i, j: (0, j)),
                  pl.BlockSpec((tf, D_MODEL), lambda i, j: (j, 0))],
        out_specs=_row_spec(tm, D_MODEL),
        out_shape=jax.ShapeDtypeStruct((m, D_MODEL), F32),
        scratch_shapes=[pltpu.VMEM((tm, D_MODEL), BF), pltpu.VMEM((tm, D_MODEL), F32)],
        compiler_params=_cparams("arbitrary", "arbitrary"),
        name="ffn",
    )(x, gn, sh, sc, g2, wg, wu, wd)


def _final_body(x_ref, g_ref, sh_ref, sc_ref, o_ref):
    o_ref[...] = _modulate(x_ref[...], g_ref[...], sh_ref[...], sc_ref[...])


def _final_call(grp, x, g, sh, sc):
    m, tm = grp.rows, grp.tm
    return pl.pallas_call(
        _final_body,
        grid=(m // tm,),
        in_specs=[_row_spec(tm, D_MODEL), _full_spec((1, D_MODEL)), grp.mod_spec(), grp.mod_spec()],
        out_specs=_row_spec(tm, D_MODEL),
        out_shape=jax.ShapeDtypeStruct((m, D_MODEL), F32),
        compiler_params=_cparams("arbitrary"),
        name="final_norm",
    )(x, g, sh, sc)


REC_MAIN = 5 * 1024


def _rec_in_body(x_ref, g_ref, sh_ref, sc_ref, w_ref, wdt_ref, o_ref, dt_ref, h_s):
    j = pl.program_id(1)

    @pl.when(j == 0)
    def _():
        h = _modulate(x_ref[...], g_ref[...], sh_ref[...], sc_ref[...]).astype(BF)
        h_s[...] = h
        dt_ref[...] = _dot(h, wdt_ref[...])

    o_ref[...] = _dot(h_s[...], w_ref[...])


def _rec_in_call(grp, x, g, sh, sc, w_main, w_dt):
    m, tm = grp.rows, grp.tm
    tn = 1024
    return pl.pallas_call(
        _rec_in_body,
        grid=(m // tm, REC_MAIN // tn),
        in_specs=[_row_spec(tm, D_MODEL), _full_spec((1, D_MODEL)), grp.mod_spec(), grp.mod_spec(),
                  pl.BlockSpec((D_MODEL, tn), lambda i, j: (0, j)), _full_spec((D_MODEL, LANES))],
        out_specs=[pl.BlockSpec((tm, tn), lambda i, j: (i, j)), _row_spec(tm, LANES)],
        out_shape=[jax.ShapeDtypeStruct((m, REC_MAIN), F32), jax.ShapeDtypeStruct((m, LANES), F32)],
        scratch_shapes=[pltpu.VMEM((tm, D_MODEL), BF)],
        compiler_params=_cparams("arbitrary", "arbitrary"),
        name="rec_in",
    )(x, g, sh, sc, w_main, w_dt)


def _causal_conv(ext_ref, x, buf_ref, w_ref, b_ref, first):
    t = x.shape[0]

    @pl.when(first)
    def _():
        ext_ref[0:8, :] = buf_ref[...]

    ext_ref[8:8 + t, :] = x
    y = b_ref[...] + x * w_ref[CONV_WIDTH - 1:CONV_WIDTH, :]
    for back in range(1, CONV_WIDTH):
        k = CONV_WIDTH - 1 - back
        y = y + ext_ref[pl.ds(8 - back, t), :] * w_ref[k:k + 1, :]
    ext_ref[0:8, :] = ext_ref[t:t + 8, :]
    return y


def _pair_expand(v, npairs):
    rows = v.shape[0]
    lane = lax.broadcasted_iota(jnp.int32, (rows, LANES), 1)
    parts = [jnp.where(lane < HEAD_DIM, v[:, 2 * p:2 * p + 1], v[:, 2 * p + 1:2 * p + 2]) for p in range(npairs)]
    return jnp.concatenate(parts, axis=1)


def _ssd_body(z_ref, xs_ref, bc_ref, dt_ref, h0_ref, bufx_ref, bufbc_ref, cwx_ref, cbx_ref, cwbc_ref, cbbc_ref,
              dtb_ref, alog_ref, dexp_ref, nw_ref, y_ref, hout_ref, extx, extbc, hst):
    q = CHUNK
    c = pl.program_id(1)
    first = c == 0

    @pl.when(first)
    def _():
        hst[...] = h0_ref[...]

    xs = jax.nn.silu(_causal_conv(extx, xs_ref[...], bufx_ref, cwx_ref, cbx_ref, first))
    bc = jax.nn.silu(_causal_conv(extbc, bc_ref[...], bufbc_ref, cwbc_ref, cbbc_ref, first))
    gw = SSD_GROUPS * SSD_STATE
    bm = bc[:, 0:gw].astype(BF)
    cm = bc[:, gw:2 * gw].astype(BF)

    dt = _softplus(dt_ref[...] + dtb_ref[...])
    a = dt * (-jnp.exp(alog_ref[...]))
    row = lax.broadcasted_iota(jnp.int32, (q, q), 0)
    col = lax.broadcasted_iota(jnp.int32, (q, q), 1)
    causal = col <= row
    tri = causal.astype(BF)
    a1, a2, a3 = _split3(a)
    cs = _dot(tri, a1) + _dot(tri, a2) + _dot(tri, a3)
    cs_t = cs.T
    cs_last = cs[q - 1:q, :]
    dend = jnp.exp(cs_last - cs)
    ecs = jnp.exp(cs)
    chunk_decay_t = jnp.exp(cs_t[:, q - 1:q])

    npairs = SSD_HEADS // 2
    xdt = xs * _pair_expand(dt, npairs)
    xdt16 = xdt.astype(BF)
    xd_t = (xdt * _pair_expand(dend, npairs)).T.astype(BF)
    ecs_x = _pair_expand(ecs, npairs)
    lane = lax.broadcasted_iota(jnp.int32, (q, LANES), 1)
    srow = lax.broadcasted_iota(jnp.int32, (LANES, SSD_STATE), 0)
    heads_per_group = SSD_HEADS // SSD_GROUPS
    ys = []
    for p in range(npairs):
        g = (2 * p) // heads_per_group
        bg = bm[:, g * SSD_STATE:(g + 1) * SSD_STATE]
        cg = cm[:, g * SSD_STATE:(g + 1) * SSD_STATE]
        cb = _dot_nt(cg, bg)
        xp = xdt16[:, p * LANES:(p + 1) * LANES]
        zero = jnp.zeros_like(xp)
        y = None
        for hh in range(2):
            h = 2 * p + hh
            seg = cs[:, h:h + 1] - cs_t[h:h + 1, :]
            decay = jnp.where(causal, jnp.exp(jnp.where(causal, seg, 0.0)), 0.0)
            xh = jnp.where((lane < HEAD_DIM) == (hh == 0), xp, zero)
            part = _dot((cb * decay).astype(BF), xh)
            y = part if y is None else y + part
        hp = hst[p]
        y = y + _dot_nt(cg, hp.astype(BF)) * ecs_x[:, p * LANES:(p + 1) * LANES]
        states = _dot(xd_t[p * LANES:(p + 1) * LANES, :], bg)
        cdec = jnp.where(srow < HEAD_DIM, chunk_decay_t[2 * p:2 * p + 1, :], chunk_decay_t[2 * p + 1:2 * p + 2, :])
        hst[p] = hp * cdec + states
        ys.append(y)
    y = jnp.concatenate(ys, axis=1)
    y = y + dexp_ref[...] * xs
    y = y * jax.nn.silu(z_ref[...])
    gsz = SSD_WIDTH // SSD_GROUPS
    outs = []
    for g in range(SSD_GROUPS):
        yg = y[:, g * gsz:(g + 1) * gsz]
        outs.append(yg * lax.rsqrt(jnp.mean(yg * yg, axis=-1, keepdims=True) + EPS))
    y_ref[...] = (jnp.concatenate(outs, axis=1) * nw_ref[...]).astype(BF)

    @pl.when(c == pl.num_programs(1) - 1)
    def _():
        hout_ref[...] = hst[...]


def _ssd_call(batch, length, proj, dt, h0, bufx, bufbc, cwx, cbx, cwbc, cbbc, dtb, alog, dexp, nw):
    q = CHUNK
    nc = length // q
    col = lambda j: pl.BlockSpec((q, 1024), lambda b, c: (b * nc + c, j))
    npairs = SSD_HEADS // 2
    st_spec = pl.BlockSpec((None, npairs, LANES, SSD_STATE), lambda b, c: (b, 0, 0, 0))
    buf_spec = pl.BlockSpec((None, 8, 1024), lambda b, c: (b, 0, 0))
    return pl.pallas_call(
        _ssd_body,
        grid=(batch, nc),
        in_specs=[col(0), col(1), col(2), pl.BlockSpec((q, LANES), lambda b, c: (b * nc + c, 0)),
                  st_spec, buf_spec, buf_spec,
                  _full_spec((CONV_WIDTH, 1024)), _full_spec((1, 1024)), _full_spec((CONV_WIDTH, 1024)),
                  _full_spec((1, 1024)), _full_spec((1, LANES)), _full_spec((1, LANES)),
                  _full_spec((1, 1024)), _full_spec((1, 1024))],
        out_specs=[pl.BlockSpec((q, 1024), lambda b, c: (b * nc + c, 0)), st_spec],
        out_shape=[jax.ShapeDtypeStruct((batch * length, 1024), BF),
                   jax.ShapeDtypeStruct((batch, npairs, LANES, SSD_STATE), F32)],
        scratch_shapes=[pltpu.VMEM((q + 8, 1024), F32), pltpu.VMEM((q + 8, 1024), F32),
                        pltpu.VMEM((npairs, LANES, SSD_STATE), F32)],
        compiler_params=_cparams("arbitrary", "arbitrary"),
        name="ssd",
    )(proj, proj, proj, dt, h0, bufx, bufbc, cwx, cbx, cwbc, cbbc, dtb, alog, dexp, nw)


def _lru_body(gate_ref, xl_ref, h0_ref, buf_ref, cw_ref, cb_ref, wa_ref, ba_ref, wx_ref, bx_ref, lam_ref,
              y_ref, hout_ref, ext, hprev, *, last_chunk, last_row):
    t = CHUNK
    c = pl.program_id(1)
    first = c == 0

    @pl.when(first)
    def _():
        hprev[...] = h0_ref[...]

    x = _causal_conv(ext, xl_ref[...], buf_ref, cw_ref, cb_ref, first)
    x16 = x.astype(BF)
    nb = LRU_WIDTH // LANES
    ra = jnp.concatenate([_dot(x16[:, p * LANES:(p + 1) * LANES], wa_ref[p]) for p in range(nb)], axis=1)
    rx = jnp.concatenate([_dot(x16[:, p * LANES:(p + 1) * LANES], wx_ref[p]) for p in range(nb)], axis=1)
    r = jax.nn.sigmoid(ra + ba_ref[...])
    i = jax.nn.sigmoid(rx + bx_ref[...])
    log_a = -LRU_C * r * _softplus(-lam_ref[...])
    a = jnp.exp(log_a)
    u = jnp.sqrt(1.0 - jnp.exp(2.0 * log_a)) * (i * x)
    row = lax.broadcasted_iota(jnp.int32, (t, LRU_WIDTH), 0)
    sh = 1
    while sh < t:
        m = row >= sh
        u = jnp.where(m, a * pltpu.roll(u, sh, 0) + u, u)
        a = jnp.where(m, a * pltpu.roll(a, sh, 0), a)
        sh *= 2
    hs = a * hprev[...] + u
    hprev[...] = hs[t - 1:t, :]
    y_ref[...] = (hs * jax.nn.gelu(gate_ref[...])).astype(BF)

    @pl.when(c == last_chunk)
    def _():
        hout_ref[...] = hs[last_row:last_row + 1, :]


def _lru_call(batch, length, n_real, proj, h0, buf, cw, cb, wa, ba, wx, bx, lam):
    t = CHUNK
    nc = length // t
    col = lambda j: pl.BlockSpec((t, 1024), lambda b, c: (b * nc + c, j))
    vec = pl.BlockSpec((None, 1, LRU_WIDTH), lambda b, c: (b, 0, 0))
    nb = LRU_WIDTH // LANES
    last = n_real - 1
    return pl.pallas_call(
        functools.partial(_lru_body, last_chunk=last // t, last_row=last % t),
        grid=(batch, nc),
        in_specs=[col(3), col(4), vec, pl.BlockSpec((None, 8, 1024), lambda b, c: (b, 0, 0)),
                  _full_spec((CONV_WIDTH, 1024)), _full_spec((1, 1024)),
                  _full_spec((nb, LANES, LANES)), _full_spec((1, 1024)),
                  _full_spec((nb, LANES, LANES)), _full_spec((1, 1024)), _full_spec((1, 1024))],
        out_specs=[pl.BlockSpec((t, 1024), lambda b, c: (b * nc + c, 0)), vec],
        out_shape=[jax.ShapeDtypeStruct((batch * length, 1024), BF),
                   jax.ShapeDtypeStruct((batch, 1, LRU_WIDTH), F32)],
        scratch_shapes=[pltpu.VMEM((t + 8, 1024), F32), pltpu.VMEM((1, LRU_WIDTH), F32)],
        compiler_params=_cparams("arbitrary", "arbitrary"),
        name="rg_lru",
    )(proj, proj, h0, buf, cw, cb, wa, ba, wx, bx, lam)


def _rope_tables(pos):
    half = HEAD_DIM // 2
    inv = ROPE_THETA ** (-jnp.arange(half, dtype=F32) / half)
    ang = pos.astype(F32)[:, None] * inv[None, :]
    cos = jnp.cos(ang)
    sin = jnp.sin(ang)
    cos = jnp.concatenate([cos, cos, cos, cos], axis=1)
    sin = jnp.concatenate([-sin, sin, -sin, sin], axis=1)
    return cos, sin


def _block_diag_pairs(w):
    nb = w.shape[0] // 2
    w2 = w.reshape(nb, 2, HEAD_DIM, HEAD_DIM)
    z = jnp.zeros((nb, HEAD_DIM, HEAD_DIM), w.dtype)
    top = jnp.concatenate([w2[:, 0], z], axis=2)
    bot = jnp.concatenate([z, w2[:, 1]], axis=2)
    return jnp.concatenate([top, bot], axis=1)


def _buf8(buf):
    return jnp.pad(buf, ((0, 0), (8 - (CONV_WIDTH - 1), 0), (0, 0)))


def _pad128(v):
    return jnp.pad(v, (0, LANES - v.shape[0])).reshape(1, LANES)


def kernel(x_prompt, x_sample, c_prompt, c_sample, cache_sb_k, cache_sb_v, cache_diff_k, cache_diff_v, state_ssd, state_ssd_conv, state_lru, state_lru_conv, page_table, w_ada, b_ada, norm_mix, norm_ffn, attn_w_in, attn_w_out, diff_lam, diff_subln, rec_w_in, ssd_conv_w, ssd_conv_b, ssd_dt_bias, ssd_a_log, ssd_d, ssd_norm, lru_conv_w, lru_conv_b, lru_w_a, lru_b_a, lru_w_x, lru_b_x, lru_lam, rec_w_out, ffn_w_gate, ffn_w_up, ffn_w_down, w_ada_final, b_ada_final, norm_final):
    bp, lp, d = x_prompt.shape
    bs, ls, _ = x_sample.shape
    n_pages = page_table.shape[1]
    n_phys = cache_sb_k.shape[1]
    na = attn_w_in.shape[0]
    nr = rec_w_in.shape[0]
    gp = _Group(bp, lp, per_row=False)
    gs = _Group(bs, ls, per_row=True)

    nb_all = bp + bs
    mp = -(-nb_all // 8) * 8
    c_all = jnp.pad(jnp.concatenate([c_prompt, c_sample], axis=0), ((0, mp - nb_all), (0, 0)))
    mods = _ada_call(c_all, w_ada, b_ada.reshape(DEPTH, 1, 6 * d))
    modf = _ada_call(c_all, w_ada_final[None], b_ada_final.reshape(1, 1, 2 * d))[0]

    def group_mods(m2d, n):
        parts = jnp.split(m2d, n, axis=-1)
        p_parts = [v[:bp].reshape(bp, 1, d) for v in parts]
        s_parts = [jnp.repeat(v[bp:bp + bs], ls, axis=0) for v in parts]
        return p_parts, s_parts

    w_attn_in = attn_w_in.astype(BF)
    w_attn_out = attn_w_out.astype(BF)
    w_rec_out = rec_w_out.astype(BF)
    wg = ffn_w_gate.astype(BF)
    wu = ffn_w_up.astype(BF)
    wd = ffn_w_down.astype(BF)
    s0, s1, s2 = 1024, 3072, 3088
    w_rec_main = jnp.concatenate([rec_w_in[:, :, :s1], rec_w_in[:, :, s2:]], axis=2).astype(BF)
    w_rec_dt = jnp.pad(rec_w_in[:, :, s1:s2], ((0, 0), (0, 0), (0, LANES - SSD_HEADS))).astype(BF)
    wa_bd = jax.vmap(_block_diag_pairs)(lru_w_a).astype(BF)
    wx_bd = jax.vmap(_block_diag_pairs)(lru_w_x).astype(BF)

    cos_p, sin_p = _rope_tables(jnp.arange(lp))
    pos_s = n_pages * PAGE + jnp.arange(ls)
    cos_s, sin_s = _rope_tables(jnp.tile(pos_s, bs))
    pt_flat = page_table.reshape(-1).astype(jnp.int32)

    kt_shape = (na * n_phys, N_CLS, HEAD_DIM, PAGE)
    kc_sb = cache_sb_k.transpose(0, 1, 3, 4, 2).reshape(kt_shape)
    vc_sb = cache_sb_v.transpose(0, 1, 3, 4, 2).reshape(kt_shape)
    kc_d = cache_diff_k.transpose(0, 1, 3, 4, 5, 2).reshape(kt_shape)
    vc_d = cache_diff_v.reshape(na * n_phys, PAGE * DIFF_HEADS, LANES)

    def q_rows(q):
        q4 = q.reshape(bs, ls, N_CLS, HEAD_DIM).transpose(0, 2, 1, 3)
        return jnp.pad(q4, ((0, 0), (0, 0), (0, N_QPAD - ls), (0, 0))).reshape(bs, Q_ROWS, HEAD_DIM)

    def new_page_t(k):
        k4 = k.reshape(bs, ls, N_CLS, HEAD_DIM).transpose(0, 2, 3, 1)
        return jnp.pad(k4, ((0, 0), (0, 0), (0, 0), (0, PAGE - ls)))

    def from_rows(o, n_cls):
        o4 = o.reshape(bs, n_cls, N_QPAD, -1).transpose(0, 2, 1, 3)[:, :ls]
        return o4.reshape(bs * ls, SB_WIDTH).astype(BF)

    xp = x_prompt.reshape(bp * lp, d)
    xs = x_sample.reshape(bs * ls, d)
    out_p = {k: [] for k in ("sbk", "sbv", "dk", "dv", "ssd", "ssdc", "lru", "lruc")}
    out_s = {k: [] for k in ("sbk", "sbv", "dk", "dv", "ssd", "ssdc", "lru", "lruc")}
    lpad = CHUNK

    for li in range(DEPTH):
        (sh1p, sc1p, g1p, sh2p, sc2p, g2p), (sh1s, sc1s, g1s, sh2s, sc2s, g2s) = group_mods(mods[li], 6)
        gmix = norm_mix[li].reshape(1, d)
        if li % 2 == 0:
            ai = li // 2
            lam_init = 0.8 - 0.6 * math.exp(-0.3 * li)
            gain = diff_subln[ai].reshape(1, LANES)
            qsb, ksb, ksbh, vsb, vsbh, qd, kd, kdh, vd, vdh = _attn_in_call(
                gp, xp, gmix, sh1p, sc1p, w_attn_in[ai], cos_p, sin_p)
            o_sb = _sb_prompt_call(bp, lp, qsb, ksbh, vsbh)
            o_d = _diff_prompt_call(bp, lp, qd, kdh, vdh, diff_lam[ai], gain, lam_init)
            xp = _out_proj_call(gp, o_sb, o_d, w_attn_out[ai], xp, g1p)
            out_p["sbk"].append(ksb.reshape(bp, lp, SB_HEADS, HEAD_DIM))
            out_p["sbv"].append(vsb.reshape(bp, lp, SB_HEADS, HEAD_DIM))
            out_p["dk"].append(kd.reshape(bp, lp, DIFF_HEADS, 2, HEAD_DIM))
            out_p["dv"].append(vd.reshape(bp, lp, DIFF_HEADS, 2 * HEAD_DIM))
            qsb, ksb, _, vsb, _, qd, kd, _, vd, _ = _attn_in_call(
                gs, xs, gmix, sh1s, sc1s, w_attn_in[ai], cos_s, sin_s)

            base = ai * n_phys
            o = _sb_sample_call(pt_flat, n_pages, base, q_rows(qsb), kc_sb, vc_sb,
                                new_page_t(ksb), new_page_t(vsb))
            o_sb = from_rows(o, N_CLS)
            vd_page = jnp.pad(vd.reshape(bs, ls * DIFF_HEADS, LANES),
                              ((0, 0), (0, (PAGE - ls) * DIFF_HEADS), (0, 0)))
            o = _diff_sample_call(pt_flat, n_pages, base, q_rows(qd), kc_d, vc_d,
                                  new_page_t(kd), vd_page, diff_lam[ai], gain, lam_init)
            o_d = from_rows(o, DIFF_HEADS)
            xs = _out_proj_call(gs, o_sb, o_d, w_attn_out[ai], xs, g1s)
            out_s["sbk"].append(ksb.reshape(bs, ls, SB_HEADS, HEAD_DIM))
            out_s["sbv"].append(vsb.reshape(bs, ls, SB_HEADS, HEAD_DIM))
            out_s["dk"].append(kd.reshape(bs, ls, DIFF_HEADS, 2, HEAD_DIM))
            out_s["dv"].append(vd.reshape(bs, ls, DIFF_HEADS, 2 * HEAD_DIM))
        else:
            ri = li // 2
            cw = ssd_conv_w[ri]
            cb = ssd_conv_b[ri].reshape(1, -1)
            ssd_args = (cw[:, :1024], cb[:, :1024], cw[:, 1024:], cb[:, 1024:],
                        _pad128(ssd_dt_bias[ri]), _pad128(ssd_a_log[ri]),
                        jnp.repeat(ssd_d[ri], HEAD_DIM).reshape(1, SSD_WIDTH), ssd_norm[ri].reshape(1, SSD_WIDTH))
            lru_args = (lru_conv_w[ri], lru_conv_b[ri].reshape(1, -1), wa_bd[ri], lru_b_a[ri].reshape(1, -1),
                        wx_bd[ri], lru_b_x[ri].reshape(1, -1), lru_lam[ri].reshape(1, -1))
            npairs = SSD_HEADS // 2
            proj, dt = _rec_in_call(gp, xp, gmix, sh1p, sc1p, w_rec_main[ri], w_rec_dt[ri])
            zeros8 = jnp.zeros((bp, 8, 1024), F32)
            y_ssd, h_ssd = _ssd_call(bp, lp, proj, dt, jnp.zeros((bp, npairs, LANES, SSD_STATE), F32),
                                     zeros8, zeros8, *ssd_args)
            y_lru, h_lru = _lru_call(bp, lp, lp, proj, jnp.zeros((bp, 1, LRU_WIDTH), F32), zeros8, *lru_args)
            xp = _out_proj_call(gp, y_ssd, y_lru, w_rec_out[ri], xp, g1p)
            p3 = proj.reshape(bp, lp, REC_MAIN)
            out_p["ssd"].append(h_ssd.reshape(bp, SSD_HEADS, HEAD_DIM, SSD_STATE))
            out_p["ssdc"].append(p3[:, lp - (CONV_WIDTH - 1):, s0:s1])
            out_p["lru"].append(h_lru.reshape(bp, LRU_WIDTH))
            out_p["lruc"].append(p3[:, lp - (CONV_WIDTH - 1):, 4096:])
            proj, dt = _rec_in_call(gs, xs, gmix, sh1s, sc1s, w_rec_main[ri], w_rec_dt[ri])
            p3 = proj.reshape(bs, ls, REC_MAIN)
            proj_pad = jnp.pad(p3, ((0, 0), (0, lpad - ls), (0, 0))).reshape(bs * lpad, REC_MAIN)
            dt_pad = jnp.pad(dt.reshape(bs, ls, LANES), ((0, 0), (0, lpad - ls), (0, 0)),
                             constant_values=-jnp.inf).reshape(bs * lpad, LANES)
            sbuf = _buf8(state_ssd_conv[ri])
            y_ssd, h_ssd = _ssd_call(bs, lpad, proj_pad, dt_pad,
                                     state_ssd[ri].reshape(bs, npairs, LANES, SSD_STATE),
                                     sbuf[:, :, :1024], sbuf[:, :, 1024:], *ssd_args)
            y_lru, h_lru = _lru_call(bs, lpad, ls, proj_pad, state_lru[ri].reshape(bs, 1, LRU_WIDTH),
                                     _buf8(state_lru_conv[ri]), *lru_args)
            y_ssd = y_ssd.reshape(bs, lpad, SSD_WIDTH)[:, :ls].reshape(bs * ls, SSD_WIDTH)
            y_lru = y_lru.reshape(bs, lpad, LRU_WIDTH)[:, :ls].reshape(bs * ls, LRU_WIDTH)
            xs = _out_proj_call(gs, y_ssd, y_lru, w_rec_out[ri], xs, g1s)
            out_s["ssd"].append(h_ssd.reshape(bs, SSD_HEADS, HEAD_DIM, SSD_STATE))
            out_s["ssdc"].append(p3[:, ls - (CONV_WIDTH - 1):, s0:s1])
            out_s["lru"].append(h_lru.reshape(bs, LRU_WIDTH))
            out_s["lruc"].append(p3[:, ls - (CONV_WIDTH - 1):, 4096:])
        gffn = norm_ffn[li].reshape(1, d)
        xp = _ffn_call(gp, xp, gffn, sh2p, sc2p, g2p, wg[li], wu[li], wd[li])
        xs = _ffn_call(gs, xs, gffn, sh2s, sc2s, g2s, wg[li], wu[li], wd[li])

    (shp, scp), (shs, scs) = group_mods(modf, 2)
    gfin = norm_final.reshape(1, d)
    y_prompt = _final_call(gp, xp, gfin, shp, scp).reshape(bp, lp, d)
    y_sample = _final_call(gs, xs, gfin, shs, scs).reshape(bs, ls, d)

    order = ("sbk", "sbv", "dk", "dv", "ssd", "ssdc", "lru", "lruc")
    return (y_prompt, y_sample) + tuple(jnp.stack(out_p[k]) for k in order) + tuple(jnp.stack(out_s[k]) for k in order)
```

```python
import functools
import math

import jax
import jax.numpy as jnp
from jax import lax
from jax.experimental import pallas as pl
from jax.experimental.pallas import tpu as pltpu

BF = jnp.bfloat16
F32 = jnp.float32

D_MODEL = 1024
DEPTH = 4
PAGE = 128
HEAD_DIM = 64
SB_HEADS = 8
SB_WIDTH = SB_HEADS * HEAD_DIM
DIFF_HEADS = 4
ATTN_IN = 6 * SB_WIDTH
ROPE_THETA = 10000.0
NEG_INF = -1e30
SSD_HEADS = 16
SSD_WIDTH = 1024
SSD_GROUPS = 4
SSD_STATE = 128
CONV_WIDTH = 4
LRU_WIDTH = 1024
LRU_BLOCKS = 16
LRU_C = 8.0
FFN_HIDDEN = 2816
EPS = 1e-6

LANES = 128
ROW_TILE = 512
ATT_TQ = 256
ATT_TK = 512
CHUNK = 128
FFN_TILE = 1408
VMEM_LIMIT = 52 * 1024 * 1024


def _cparams(*sem):
    return pltpu.CompilerParams(dimension_semantics=sem, vmem_limit_bytes=VMEM_LIMIT)


def _dot(a, b):
    return jnp.dot(a, b, preferred_element_type=F32)


def _dot_nt(a, b):
    return lax.dot_general(a, b, (((1,), (1,)), ((), ())), preferred_element_type=F32)


def _softplus(x):
    return jnp.maximum(x, 0.0) + jnp.log1p(jnp.exp(-jnp.abs(x)))


def _log_sigmoid(z):
    return jnp.minimum(z, 0.0) - jnp.log(1.0 + jnp.exp(-jnp.abs(z)))


def _modulate(x, g, sh, sc):
    y = x * lax.rsqrt(jnp.mean(x * x, axis=-1, keepdims=True) + EPS) * g
    return y * (1.0 + sc) + sh


def _split3(x):
    a = x.astype(BF)
    r = x - a.astype(F32)
    b = r.astype(BF)
    c = (r - b.astype(F32)).astype(BF)
    return a, b, c


def _split2(x):
    a = x.astype(BF)
    b = (x - a.astype(F32)).astype(BF)
    return a, b


def _ada_body(c_ref, w_ref, b_ref, o_ref):
    h = jax.nn.silu(c_ref[...]).astype(BF)
    o_ref[...] = _dot(h, w_ref[...].astype(BF)) + b_ref[...]


def _ada_call(c_all, w, b):
    nl, d, n = w.shape
    mp = c_all.shape[0]
    tn = 1024
    return pl.pallas_call(
        _ada_body,
        grid=(nl, n // tn),
        in_specs=[pl.BlockSpec((mp, d), lambda l, j: (0, 0)),
                  pl.BlockSpec((None, d, tn), lambda l, j: (l, 0, j)),
                  pl.BlockSpec((None, 1, tn), lambda l, j: (l, 0, j))],
        out_specs=pl.BlockSpec((None, mp, tn), lambda l, j: (l, 0, j)),
        out_shape=jax.ShapeDtypeStruct((nl, mp, n), F32),
        compiler_params=_cparams("arbitrary", "arbitrary"),
        name="ada",
    )(c_all, w, b)


class _Group:
    def __init__(self, batch, length, per_row):
        self.batch = batch
        self.length = length
        self.rows = batch * length
        self.per_row = per_row
        self.tm = min(ROW_TILE, self.rows)
        self.blocks_per_seq = 1 if per_row else length // self.tm

    def mod_spec(self, d=D_MODEL):
        if self.per_row:
            return pl.BlockSpec((self.tm, d), lambda i, *_: (i, 0))
        bps = self.blocks_per_seq
        return pl.BlockSpec((None, 1, d), lambda i, *_: (i // bps, 0, 0))

    def pos_spec(self):
        if self.per_row:
            return pl.BlockSpec((self.tm, LANES), lambda i, *_: (i, 0))
        bps = self.blocks_per_seq
        return pl.BlockSpec((self.tm, LANES), lambda i, *_: (i % bps, 0))


def _row_spec(tm, n):
    return pl.BlockSpec((tm, n), lambda i, *_: (i, 0))


def _full_spec(shape):
    nd = len(shape)
    return pl.BlockSpec(shape, lambda *_: (0,) * nd)


def _rope(x, cos, sin):
    n = x.shape[1] // LANES
    c = jnp.concatenate([cos] * n, axis=1)
    s = jnp.concatenate([sin] * n, axis=1)
    lane = lax.broadcasted_iota(jnp.int32, x.shape, 1)
    first = (lane % HEAD_DIM) < (HEAD_DIM // 2)
    w = x.shape[1]
    partner = jnp.where(first, pltpu.roll(x, w - HEAD_DIM // 2, 1), pltpu.roll(x, HEAD_DIM // 2, 1))
    return x * c + partner * s


def _attn_in_body(x_ref, g_ref, sh_ref, sc_ref, w_ref, cos_ref, sin_ref,
                  qsb_ref, ksb_ref, ksbh_ref, vsb_ref, vsbh_ref, qd_ref, kd_ref, kdh_ref, vd_ref, vdh_ref):
    h = _modulate(x_ref[...], g_ref[...], sh_ref[...], sc_ref[...]).astype(BF)
    proj = _dot(h, w_ref[...])
    w = SB_WIDTH
    scale = HEAD_DIM ** -0.5
    qsb_ref[...] = (proj[:, 0:w] * scale).astype(BF)
    k = proj[:, w:2 * w]
    ksb_ref[...] = k
    ksbh_ref[...] = k.astype(BF)
    v = proj[:, 2 * w:3 * w]
    vsb_ref[...] = v
    vsbh_ref[...] = v.astype(BF)
    cos = cos_ref[...]
    sin = sin_ref[...]
    qd_ref[...] = (_rope(proj[:, 3 * w:4 * w], cos, sin) * scale).astype(BF)
    kd = _rope(proj[:, 4 * w:5 * w], cos, sin)
    kd_ref[...] = kd
    kdh_ref[...] = kd.astype(BF)
    vd = proj[:, 5 * w:6 * w]
    vd_ref[...] = vd
    vdh_ref[...] = vd.astype(BF)


def _attn_in_call(grp, x, g, sh, sc, w, cos, sin):
    m, tm = grp.rows, grp.tm
    o32 = jax.ShapeDtypeStruct((m, SB_WIDTH), F32)
    o16 = jax.ShapeDtypeStruct((m, SB_WIDTH), BF)
    spec = _row_spec(tm, SB_WIDTH)
    return pl.pallas_call(
        _attn_in_body,
        grid=(m // tm,),
        in_specs=[_row_spec(tm, D_MODEL), _full_spec((1, D_MODEL)), grp.mod_spec(), grp.mod_spec(),
                  _full_spec((D_MODEL, ATTN_IN)), grp.pos_spec(), grp.pos_spec()],
        out_specs=[spec] * 10,
        out_shape=[o16, o32, o16, o32, o16, o16, o32, o16, o32, o16],
        compiler_params=_cparams("arbitrary"),
        name="attn_in",
    )(x, g, sh, sc, w, cos, sin)


def _sb_prompt_body(q_ref, k_ref, v_ref, o_ref):
    tq, tk = ATT_TQ, ATT_TK
    qi = pl.program_id(2)
    q = q_ref[...]
    lane = lax.broadcasted_iota(jnp.int32, (tq, LANES), 1)
    zero = jnp.zeros_like(q)
    qs = (jnp.where(lane < HEAD_DIM, q, zero), jnp.where(lane >= HEAD_DIM, q, zero))
    krow = lax.broadcasted_iota(jnp.int32, (tk, tk), 0)
    kcol = lax.broadcasted_iota(jnp.int32, (tk, tk), 1)
    later = (krow > kcol).astype(BF)
    row = lax.broadcasted_iota(jnp.int32, (tq, tk), 0) + qi * tq
    col = lax.broadcasted_iota(jnp.int32, (tq, tk), 1)

    def tile(kb, carry, masked):
        start = pl.multiple_of(kb * tk, tk)
        k = k_ref[pl.ds(start, tk), :]
        v = v_ref[pl.ds(start, tk), :]
        out = []
        for qh, (run, acc) in zip(qs, carry):
            z = _dot_nt(qh, k)
            lb = _log_sigmoid(z)
            lk = lb - z
            if masked:
                valid = col + kb * tk < row
                lk = jnp.where(valid, lk, 0.0)
            hi, lo = _split2(lk)
            rest = _dot(hi, later) + _dot(lo, later) + run
            wgt = jnp.exp(lb + rest)
            if masked:
                wgt = jnp.where(valid, wgt, 0.0)
            acc = acc + _dot(wgt.astype(BF), v)
            run = run + jnp.sum(lk, axis=1, keepdims=True)
            out.append((run, acc))
        return tuple(out)

    init = tuple((jnp.zeros((tq, 1), F32), jnp.zeros((tq, LANES), F32)) for _ in range(2))
    kd = (qi * tq) // tk
    carry = tile(kd, init, True)
    carry = lax.fori_loop(0, kd, lambda i, c: tile(kd - 1 - i, c, False), carry)
    o_ref[...] = jnp.where(lane < HEAD_DIM, carry[0][1], carry[1][1]).astype(BF)


def _prompt_attn_specs(batch, length):
    tq = ATT_TQ
    assert ATT_TK % tq == 0 and length % ATT_TK == 0
    nq = length // tq
    q_spec = pl.BlockSpec((tq, LANES), lambda b, p, i: (b * nq + i, p))
    kv_spec = pl.BlockSpec((length, LANES), lambda b, p, i: (b, p))
    return nq, q_spec, kv_spec


def _sb_prompt_call(batch, length, q, k, v):
    nq, q_spec, kv_spec = _prompt_attn_specs(batch, length)
    return pl.pallas_call(
        _sb_prompt_body,
        grid=(batch, SB_WIDTH // LANES, nq),
        in_specs=[q_spec, kv_spec, kv_spec],
        out_specs=q_spec,
        out_shape=jax.ShapeDtypeStruct(q.shape, BF),
        compiler_params=_cparams("arbitrary", "arbitrary", "arbitrary"),
        name="sb_prompt",
    )(q, k, v)


def _diff_lambda(lam_ref, lam_init):
    lv = lam_ref[...]
    s1 = jnp.sum(lv[0:1] * lv[1:2], axis=1, keepdims=True)
    s2 = jnp.sum(lv[2:3] * lv[3:4], axis=1, keepdims=True)
    return jnp.exp(s1) - jnp.exp(s2) + lam_init


def _diff_finish(o0, o1, lam, gain, lam_init):
    o = o0 - lam * o1
    y = o * lax.rsqrt(jnp.mean(o * o, axis=-1, keepdims=True) + EPS) * gain
    return y * (1.0 - lam_init)


def _diff_prompt_body(q_ref, k_ref, v_ref, lam_ref, gain_ref, o_ref, *, lam_init):
    tq, tk = ATT_TQ, ATT_TK
    qi = pl.program_id(2)
    q = q_ref[...]
    lane = lax.broadcasted_iota(jnp.int32, (tq, LANES), 1)
    zero = jnp.zeros_like(q)
    qs = (jnp.where(lane < HEAD_DIM, q, zero), jnp.where(lane >= HEAD_DIM, q, zero))
    row = lax.broadcasted_iota(jnp.int32, (tq, tk), 0) + qi * tq
    col = lax.broadcasted_iota(jnp.int32, (tq, tk), 1)

    def tile(kb, carry, masked):
        start = pl.multiple_of(kb * tk, tk)
        k = k_ref[pl.ds(start, tk), :]
        v = v_ref[pl.ds(start, tk), :]
        out = []
        for qh, (mx, den, acc) in zip(qs, carry):
            z = _dot_nt(qh, k)
            if masked:
                z = jnp.where(col + kb * tk <= row, z, NEG_INF)
            mx_new = jnp.maximum(mx, jnp.max(z, axis=1, keepdims=True))
            alpha = jnp.exp(mx - mx_new)
            p = jnp.exp(z - mx_new)
            den = alpha * den + jnp.sum(p, axis=1, keepdims=True)
            acc = alpha * acc + _dot(p.astype(BF), v)
            out.append((mx_new, den, acc))
        return tuple(out)

    init = tuple((jnp.full((tq, 1), NEG_INF, F32), jnp.zeros((tq, 1), F32), jnp.zeros((tq, LANES), F32))
                 for _ in range(2))
    kd = (qi * tq) // tk
    carry = tile(kd, init, True)
    carry = lax.fori_loop(0, kd, lambda i, c: tile(kd - 1 - i, c, False), carry)
    lam = _diff_lambda(lam_ref, lam_init)
    o0 = carry[0][2] / carry[0][1]
    o1 = carry[1][2] / carry[1][1]
    o_ref[...] = _diff_finish(o0, o1, lam, gain_ref[...], lam_init).astype(BF)


def _diff_prompt_call(batch, length, q, k, v, lam, gain, lam_init):
    nq, q_spec, kv_spec = _prompt_attn_specs(batch, length)
    return pl.pallas_call(
        functools.partial(_diff_prompt_body, lam_init=lam_init),
        grid=(batch, DIFF_HEADS, nq),
        in_specs=[q_spec, kv_spec, kv_spec, _full_spec((4, HEAD_DIM)), _full_spec((1, LANES))],
        out_specs=q_spec,
        out_shape=jax.ShapeDtypeStruct(q.shape, BF),
        compiler_params=_cparams("arbitrary", "arbitrary", "arbitrary"),
        name="diff_prompt",
    )(q, k, v, lam, gain)


N_CLS = 8
N_QPAD = 8
PAGES_PER_STEP = 8
Q_ROWS = N_CLS * N_QPAD


def _page_specs(n_pages, base, block):
    def spec(slot):
        def idx(b, j, pt_ref):
            page = n_pages - jnp.maximum(j, 1) * PAGES_PER_STEP + slot
            return (base + pt_ref[b * n_pages + page],) + (0,) * (len(block) - 1)
        return pl.BlockSpec(block, idx)
    return [spec(s) for s in range(PAGES_PER_STEP)]


CLS_WIDTH = N_CLS * HEAD_DIM


def _class_scores(qbd, k_refs):
    return jnp.concatenate(
        [_dot(qbd, k_ref[...].reshape(CLS_WIDTH, PAGE).astype(BF)) for k_ref in k_refs], axis=0)


def _own_block(rows_per_block, lanes_per_block):
    r = lax.broadcasted_iota(jnp.int32, (Q_ROWS, CLS_WIDTH), 0) // rows_per_block
    c = lax.broadcasted_iota(jnp.int32, (Q_ROWS, CLS_WIDTH), 1) // lanes_per_block
    return r == c


def _fold_slabs(a):
    return sum(a[:, p * LANES:(p + 1) * LANES] for p in range(a.shape[1] // LANES))


def _sb_sample_body(pt_ref, q_ref, *refs, n_steps):
    np_ = PAGES_PER_STEP
    kc, vc = refs[0:np_], refs[np_:2 * np_]
    kn_ref, vn_ref, o_ref, run_s, acc_s = refs[2 * np_:]
    j = pl.program_id(1)
    q = q_ref[...]
    row = lax.broadcasted_iota(jnp.int32, (PAGE, PAGE), 0)
    col = lax.broadcasted_iota(jnp.int32, (PAGE, PAGE), 1)
    later = (row > col).astype(BF)
    tok = lax.broadcasted_iota(jnp.int32, (Q_ROWS, PAGE), 1)
    qpos = lax.broadcasted_iota(jnp.int32, (Q_ROWS, PAGE), 0) % N_QPAD
    own = _own_block(N_QPAD, HEAD_DIM)

    def pages(k_refs, v_refs, new):
        z = _class_scores(q, k_refs)
        lb = _log_sigmoid(z)
        lk = lb - z
        if new:
            valid = tok < qpos
            lk = jnp.where(valid, lk, 0.0)
        hi, lo = _split2(lk)
        wgt = jnp.exp(lb + _dot(hi, later) + _dot(lo, later))
        if new:
            wgt = jnp.where(valid, wgt, 0.0)
        w16 = wgt.astype(BF)
        tot = jnp.sum(lk, axis=1, keepdims=True)
        parts = []
        for i, v_ref in enumerate(v_refs):
            rows = slice(i * Q_ROWS, (i + 1) * Q_ROWS)
            o_all = _dot_nt(w16[rows], v_ref[...].reshape(CLS_WIDTH, PAGE).astype(BF))
            parts.append((jnp.where(own, o_all, 0.0), tot[rows]))
        return parts

    def fold(parts):
        run, acc = run_s[...], acc_s[...]
        for pv, tot in parts:
            acc = acc + jnp.exp(run) * pv
            run = run + tot
        run_s[...] = run
        acc_s[...] = acc

    @pl.when(j == 0)
    def _():
        run_s[...] = jnp.zeros_like(run_s)
        acc_s[...] = jnp.zeros_like(acc_s)
        fold(pages([kn_ref], [vn_ref], True))

    @pl.when(j > 0)
    def _():
        fold(pages(kc[::-1], vc[::-1], False))

    @pl.when(j == n_steps - 1)
    def _():
        s = _fold_slabs(acc_s[...])
        o_ref[...] = s[:, 0:HEAD_DIM] + s[:, HEAD_DIM:LANES]


def _sb_sample_call(pt_flat, n_pages, base, q, kc, vc, kn, vn):
    nb = q.shape[0]
    n_steps = n_pages // PAGES_PER_STEP + 1
    block = (None, N_CLS, HEAD_DIM, PAGE)
    per_b = lambda *s: pl.BlockSpec((None,) + s, lambda b, j, pt: (b,) + (0,) * len(s))
    k_specs = _page_specs(n_pages, base, block)
    return pl.pallas_call(
        functools.partial(_sb_sample_body, n_steps=n_steps),
        grid_spec=pltpu.PrefetchScalarGridSpec(
            num_scalar_prefetch=1,
            grid=(nb, n_steps),
            in_specs=[per_b(Q_ROWS, CLS_WIDTH)] + k_specs + k_specs
                     + [per_b(N_CLS, HEAD_DIM, PAGE), per_b(N_CLS, HEAD_DIM, PAGE)],
            out_specs=per_b(Q_ROWS, HEAD_DIM),
            scratch_shapes=[pltpu.VMEM((Q_ROWS, 1), F32), pltpu.VMEM((Q_ROWS, CLS_WIDTH), F32)]),
        out_shape=jax.ShapeDtypeStruct((nb, Q_ROWS, HEAD_DIM), F32),
        compiler_params=_cparams("arbitrary", "arbitrary"),
        name="sb_sample",
    )(pt_flat, q, *([kc] * PAGES_PER_STEP), *([vc] * PAGES_PER_STEP), kn, vn)


def _diff_sample_body(pt_ref, q_ref, *refs, n_steps, lam_init):
    np_ = PAGES_PER_STEP
    kc, vc = refs[0:np_], refs[np_:2 * np_]
    kn_ref, vn_ref, lam_ref, gain_ref, o_ref, mx_s, den_s, acc_s = refs[2 * np_:]
    j = pl.program_id(1)
    q = q_ref[...]
    tok = lax.broadcasted_iota(jnp.int32, (Q_ROWS, PAGE), 1)
    qpos = lax.broadcasted_iota(jnp.int32, (Q_ROWS, PAGE), 0) % N_QPAD
    rows_per_head = 2 * N_QPAD
    own = _own_block(rows_per_head, LANES)

    def pages(k_refs, v_refs, new):
        z = _class_scores(q, k_refs)
        if new:
            z = jnp.where(tok <= qpos, z, NEG_INF)
        mloc = jnp.max(z, axis=1, keepdims=True)
        p = jnp.exp(z - mloc)
        p16 = p.astype(BF)
        lloc = jnp.sum(p, axis=1, keepdims=True)
        parts = []
        for i, v_ref in enumerate(v_refs):
            rows = slice(i * Q_ROWS, (i + 1) * Q_ROWS)
            v_heads = jnp.concatenate(
                [v_ref[pl.ds(h, PAGE, stride=DIFF_HEADS), :] for h in range(DIFF_HEADS)], axis=1).astype(BF)
            o_all = _dot(p16[rows], v_heads)
            parts.append((mloc[rows], lloc[rows], jnp.where(own, o_all, 0.0)))
        return parts

    def fold(parts):
        mx, den, acc = mx_s[...], den_s[...], acc_s[...]
        for mloc, lloc, pv in parts:
            mx_new = jnp.maximum(mx, mloc)
            a = jnp.exp(mx - mx_new)
            b = jnp.exp(mloc - mx_new)
            den = a * den + b * lloc
            acc = a * acc + b * pv
            mx = mx_new
        mx_s[...] = mx
        den_s[...] = den
        acc_s[...] = acc

    @pl.when(j == 0)
    def _():
        mx_s[...] = jnp.full_like(mx_s, NEG_INF)
        den_s[...] = jnp.zeros_like(den_s)
        acc_s[...] = jnp.zeros_like(acc_s)
        fold(pages([kn_ref], [vn_ref], True))

    @pl.when(j > 0)
    def _():
        fold(pages(kc[::-1], vc[::-1], False))

    @pl.when(j == n_steps - 1)
    def _():
        o = _fold_slabs(acc_s[...]) / den_s[...]
        o0 = jnp.concatenate([o[h * rows_per_head:h * rows_per_head + N_QPAD] for h in range(DIFF_HEADS)], axis=0)
        o1 = jnp.concatenate([o[h * rows_per_head + N_QPAD:(h + 1) * rows_per_head] for h in range(DIFF_HEADS)],
                             axis=0)
        lam = _diff_lambda(lam_ref, lam_init)
        o_ref[...] = _diff_finish(o0, o1, lam, gain_ref[...], lam_init)


def _diff_sample_call(pt_flat, n_pages, base, q, kc, vc, kn, vn, lam, gain, lam_init):
    nb = q.shape[0]
    n_steps = n_pages // PAGES_PER_STEP + 1
    v_rows = PAGE * DIFF_HEADS
    per_b = lambda *s: pl.BlockSpec((None,) + s, lambda b, j, pt: (b,) + (0,) * len(s))
    const = lambda shape: pl.BlockSpec(shape, lambda b, j, pt: (0,) * len(shape))
    k_specs = _page_specs(n_pages, base, (None, N_CLS, HEAD_DIM, PAGE))
    v_specs = _page_specs(n_pages, base, (None, v_rows, LANES))
    return pl.pallas_call(
        functools.partial(_diff_sample_body, n_steps=n_steps, lam_init=lam_init),
        grid_spec=pltpu.PrefetchScalarGridSpec(
            num_scalar_prefetch=1,
            grid=(nb, n_steps),
            in_specs=[per_b(Q_ROWS, CLS_WIDTH)] + k_specs + v_specs
                     + [per_b(N_CLS, HEAD_DIM, PAGE), per_b(v_rows, LANES), const((4, HEAD_DIM)), const((1, LANES))],
            out_specs=per_b(Q_ROWS // 2, LANES),
            scratch_shapes=[pltpu.VMEM((Q_ROWS, 1), F32), pltpu.VMEM((Q_ROWS, 1), F32),
                            pltpu.VMEM((Q_ROWS, CLS_WIDTH), F32)]),
        out_shape=jax.ShapeDtypeStruct((nb, Q_ROWS // 2, LANES), F32),
        compiler_params=_cparams("arbitrary", "arbitrary"),
        name="diff_sample",
    )(pt_flat, q, *([kc] * PAGES_PER_STEP), *([vc] * PAGES_PER_STEP), kn, vn, lam, gain)


def _out_proj_body(a_ref, b_ref, w_ref, x_ref, g_ref, o_ref):
    ka = a_ref.shape[1]
    y = _dot(a_ref[...], w_ref[0:ka, :]) + _dot(b_ref[...], w_ref[ka:, :])
    o_ref[...] = x_ref[...] + g_ref[...] * y


def _out_proj_call(grp, a, b, w, x, g):
    m, tm = grp.rows, grp.tm
    return pl.pallas_call(
        _out_proj_body,
        grid=(m // tm,),
        in_specs=[_row_spec(tm, a.shape[1]), _row_spec(tm, b.shape[1]), _full_spec(w.shape),
                  _row_spec(tm, D_MODEL), grp.mod_spec()],
        out_specs=_row_spec(tm, D_MODEL),
        out_shape=jax.ShapeDtypeStruct((m, D_MODEL), F32),
        compiler_params=_cparams("arbitrary"),
        name="out_proj",
    )(a, b, w, x, g)


def _ffn_body(x_ref, gn_ref, sh_ref, sc_ref, g2_ref, wg_ref, wu_ref, wd_ref, o_ref, h_s, acc_s):
    j = pl.program_id(1)

    @pl.when(j == 0)
    def _():
        h_s[...] = _modulate(x_ref[...], gn_ref[...], sh_ref[...], sc_ref[...]).astype(BF)
        acc_s[...] = jnp.zeros_like(acc_s)

    h = h_s[...]
    a = jax.nn.silu(_dot(h, wg_ref[...])) * _dot(h, wu_ref[...])
    acc_s[...] += _dot(a.astype(BF), wd_ref[...])

    @pl.when(j == pl.num_programs(1) - 1)
    def _():
        o_ref[...] = x_ref[...] + g2_ref[...] * acc_s[...]


def _ffn_call(grp, x, gn, sh, sc, g2, wg, wu, wd):
    m, tm = grp.rows, grp.tm
    tf = FFN_TILE
    return pl.pallas_call(
        _ffn_body,
        grid=(m // tm, FFN_HIDDEN // tf),
        in_specs=[_row_spec(tm, D_MODEL), _full_spec((1, D_MODEL)), grp.mod_spec(), grp.mod_spec(), grp.mod_spec(),
                  pl.BlockSpec((D_MODEL, tf), lambda i, j: (0, j)),
                  pl.BlockSpec((D_MODEL, tf), lambda i, j: (0, j)),
                  pl.BlockSpec((tf, D_MODEL), lambda i, j: (j, 0))],
        out_specs=_row_spec(tm, D_MODEL),
        out_shape=jax.ShapeDtypeStruct((m, D_MODEL), F32),
        scratch_shapes=[pltpu.VMEM((tm, D_MODEL), BF), pltpu.VMEM((tm, D_MODEL), F32)],
        compiler_params=_cparams("arbitrary", "arbitrary"),
        name="ffn",
    )(x, gn, sh, sc, g2, wg, wu, wd)


def _final_body(x_ref, g_ref, sh_ref, sc_ref, o_ref):
    o_ref[...] = _modulate(x_ref[...], g_ref[...], sh_ref[...], sc_ref[...])


def _final_call(grp, x, g, sh, sc):
    m, tm = grp.rows, grp.tm
    return pl.pallas_call(
        _final_body,
        grid=(m // tm,),
        in_specs=[_row_spec(tm, D_MODEL), _full_spec((1, D_MODEL)), grp.mod_spec(), grp.mod_spec()],
        out_specs=_row_spec(tm, D_MODEL),
        out_shape=jax.ShapeDtypeStruct((m, D_MODEL), F32),
        compiler_params=_cparams("arbitrary"),
        name="final_norm",
    )(x, g, sh, sc)


REC_MAIN = 5 * 1024


def _rec_in_body(x_ref, g_ref, sh_ref, sc_ref, w_ref, wdt_ref, o_ref, dt_ref, h_s):
    j = pl.program_id(1)

    @pl.when(j == 0)
    def _():
        h = _modulate(x_ref[...], g_ref[...], sh_ref[...], sc_ref[...]).astype(BF)
        h_s[...] = h
        dt_ref[...] = _dot(h, wdt_ref[...])

    o_ref[...] = _dot(h_s[...], w_ref[...])


def _rec_in_call(grp, x, g, sh, sc, w_main, w_dt):
    m, tm = grp.rows, grp.tm
    tn = 1024
    return pl.pallas_call(
        _rec_in_body,
        grid=(m // tm, REC_MAIN // tn),
        in_specs=[_row_spec(tm, D_MODEL), _full_spec((1, D_MODEL)), grp.mod_spec(), grp.mod_spec(),
                  pl.BlockSpec((D_MODEL, tn), lambda i, j: (0, j)), _full_spec((D_MODEL, LANES))],
        out_specs=[pl.BlockSpec((tm, tn), lambda i, j: (i, j)), _row_spec(tm, LANES)],
        out_shape=[jax.ShapeDtypeStruct((m, REC_MAIN), F32), jax.ShapeDtypeStruct((m, LANES), F32)],
        scratch_shapes=[pltpu.VMEM((tm, D_MODEL), BF)],
        compiler_params=_cparams("arbitrary", "arbitrary"),
        name="rec_in",
    )(x, g, sh, sc, w_main, w_dt)


def _causal_conv(ext_ref, x, buf_ref, w_ref, b_ref, first):
    t = x.shape[0]

    @pl.when(first)
    def _():
        ext_ref[0:8, :] = buf_ref[...]

    ext_ref[8:8 + t, :] = x
    y = b_ref[...] + x * w_ref[CONV_WIDTH - 1:CONV_WIDTH, :]
    for back in range(1, CONV_WIDTH):
        k = CONV_WIDTH - 1 - back
        y = y + ext_ref[pl.ds(8 - back, t), :] * w_ref[k:k + 1, :]
    ext_ref[0:8, :] = ext_ref[t:t + 8, :]
    return y


def _pair_expand(v, npairs):
    rows = v.shape[0]
    lane = lax.broadcasted_iota(jnp.int32, (rows, LANES), 1)
    parts = [jnp.where(lane < HEAD_DIM, v[:, 2 * p:2 * p + 1], v[:, 2 * p + 1:2 * p + 2]) for p in range(npairs)]
    return jnp.concatenate(parts, axis=1)


def _ssd_body(z_ref, xs_ref, bc_ref, dt_ref, h0_ref, bufx_ref, bufbc_ref, cwx_ref, cbx_ref, cwbc_ref, cbbc_ref,
              dtb_ref, alog_ref, dexp_ref, nw_ref, y_ref, hout_ref, extx, extbc, hst):
    q = CHUNK
    c = pl.program_id(1)
    first = c == 0

    @pl.when(first)
    def _():
        hst[...] = h0_ref[...]

    xs = jax.nn.silu(_causal_conv(extx, xs_ref[...], bufx_ref, cwx_ref, cbx_ref, first))
    bc = jax.nn.silu(_causal_conv(extbc, bc_ref[...], bufbc_ref, cwbc_ref, cbbc_ref, first))
    gw = SSD_GROUPS * SSD_STATE
    bm = bc[:, 0:gw].astype(BF)
    cm = bc[:, gw:2 * gw].astype(BF)

    dt = _softplus(dt_ref[...] + dtb_ref[...])
    a = dt * (-jnp.exp(alog_ref[...]))
    row = lax.broadcasted_iota(jnp.int32, (q, q), 0)
    col = lax.broadcasted_iota(jnp.int32, (q, q), 1)
    causal = col <= row
    tri = causal.astype(BF)
    a1, a2, a3 = _split3(a)
    cs = _dot(tri, a1) + _dot(tri, a2) + _dot(tri, a3)
    cs_t = cs.T
    cs_last = cs[q - 1:q, :]
    dend = jnp.exp(cs_last - cs)
    ecs = jnp.exp(cs)
    chunk_decay_t = jnp.exp(cs_t[:, q - 1:q])

    npairs = SSD_HEADS // 2
    xdt = xs * _pair_expand(dt, npairs)
    xdt16 = xdt.astype(BF)
    xd_t = (xdt * _pair_expand(dend, npairs)).T.astype(BF)
    ecs_x = _pair_expand(ecs, npairs)
    lane = lax.broadcasted_iota(jnp.int32, (q, LANES), 1)
    srow = lax.broadcasted_iota(jnp.int32, (LANES, SSD_STATE), 0)
    heads_per_group = SSD_HEADS // SSD_GROUPS
    ys = []
    for p in range(npairs):
        g = (2 * p) // heads_per_group
        bg = bm[:, g * SSD_STATE:(g + 1) * SSD_STATE]
        cg = cm[:, g * SSD_STATE:(g + 1) * SSD_STATE]
        cb = _dot_nt(cg, bg)
        xp = xdt16[:, p * LANES:(p + 1) * LANES]
        zero = jnp.zeros_like(xp)
        y = None
        for hh in range(2):
            h = 2 * p + hh
            seg = cs[:, h:h + 1] - cs_t[h:h + 1, :]
            decay = jnp.where(causal, jnp.exp(jnp.where(causal, seg, 0.0)), 0.0)
            xh = jnp.where((lane < HEAD_DIM) == (hh == 0), xp, zero)
            part = _dot((cb * decay).astype(BF), xh)
            y = part if y is None else y + part
        hp = hst[p]
        y = y + _dot_nt(cg, hp.astype(BF)) * ecs_x[:, p * LANES:(p + 1) * LANES]
        states = _dot(xd_t[p * LANES:(p + 1) * LANES, :], bg)
        cdec = jnp.where(srow < HEAD_DIM, chunk_decay_t[2 * p:2 * p + 1, :], chunk_decay_t[2 * p + 1:2 * p + 2, :])
        hst[p] = hp * cdec + states
        ys.append(y)
    y = jnp.concatenate(ys, axis=1)
    y = y + dexp_ref[...] * xs
    y = y * jax.nn.silu(z_ref[...])
    gsz = SSD_WIDTH // SSD_GROUPS
    outs = []
    for g in range(SSD_GROUPS):
        yg = y[:, g * gsz:(g + 1) * gsz]
        outs.append(yg * lax.rsqrt(jnp.mean(yg * yg, axis=-1, keepdims=True) + EPS))
    y_ref[...] = (jnp.concatenate(outs, axis=1) * nw_ref[...]).astype(BF)

    @pl.when(c == pl.num_programs(1) - 1)
    def _():
        hout_ref[...] = hst[...]


def _ssd_call(batch, length, proj, dt, h0, bufx, bufbc, cwx, cbx, cwbc, cbbc, dtb, alog, dexp, nw):
    q = CHUNK
    nc = length // q
    col = lambda j: pl.BlockSpec((q, 1024), lambda b, c: (b * nc + c, j))
    npairs = SSD_HEADS // 2
    st_spec = pl.BlockSpec((None, npairs, LANES, SSD_STATE), lambda b, c: (b, 0, 0, 0))
    buf_spec = pl.BlockSpec((None, 8, 1024), lambda b, c: (b, 0, 0))
    return pl.pallas_call(
        _ssd_body,
        grid=(batch, nc),
        in_specs=[col(0), col(1), col(2), pl.BlockSpec((q, LANES), lambda b, c: (b * nc + c, 0)),
                  st_spec, buf_spec, buf_spec,
                  _full_spec((CONV_WIDTH, 1024)), _full_spec((1, 1024)), _full_spec((CONV_WIDTH, 1024)),
                  _full_spec((1, 1024)), _full_spec((1, LANES)), _full_spec((1, LANES)),
                  _full_spec((1, 1024)), _full_spec((1, 1024))],
        out_specs=[pl.BlockSpec((q, 1024), lambda b, c: (b * nc + c, 0)), st_spec],
        out_shape=[jax.ShapeDtypeStruct((batch * length, 1024), BF),
                   jax.ShapeDtypeStruct((batch, npairs, LANES, SSD_STATE), F32)],
        scratch_shapes=[pltpu.VMEM((q + 8, 1024), F32), pltpu.VMEM((q + 8, 1024), F32),
                        pltpu.VMEM((npairs, LANES, SSD_STATE), F32)],
        compiler_params=_cparams("arbitrary", "arbitrary"),
        name="ssd",
    )(proj, proj, proj, dt, h0, bufx, bufbc, cwx, cbx, cwbc, cbbc, dtb, alog, dexp, nw)


def _lru_body(gate_ref, xl_ref, h0_ref, buf_ref, cw_ref, cb_ref, wa_ref, ba_ref, wx_ref, bx_ref, lam_ref,
              y_ref, hout_ref, ext, hprev, *, last_chunk, last_row):
    t = CHUNK
    c = pl.program_id(1)
    first = c == 0

    @pl.when(first)
    def _():
        hprev[...] = h0_ref[...]

    x = _causal_conv(ext, xl_ref[...], buf_ref, cw_ref, cb_ref, first)
    x16 = x.astype(BF)
    nb = LRU_WIDTH // LANES
    ra = jnp.concatenate([_dot(x16[:, p * LANES:(p + 1) * LANES], wa_ref[p]) for p in range(nb)], axis=1)
    rx = jnp.concatenate([_dot(x16[:, p * LANES:(p + 1) * LANES], wx_ref[p]) for p in range(nb)], axis=1)
    r = jax.nn.sigmoid(ra + ba_ref[...])
    i = jax.nn.sigmoid(rx + bx_ref[...])
    log_a = -LRU_C * r * _softplus(-lam_ref[...])
    a = jnp.exp(log_a)
    u = jnp.sqrt(1.0 - jnp.exp(2.0 * log_a)) * (i * x)
    row = lax.broadcasted_iota(jnp.int32, (t, LRU_WIDTH), 0)
    sh = 1
    while sh < t:
        m = row >= sh
        u = jnp.where(m, a * pltpu.roll(u, sh, 0) + u, u)
        a = jnp.where(m, a * pltpu.roll(a, sh, 0), a)
        sh *= 2
    hs = a * hprev[...] + u
    hprev[...] = hs[t - 1:t, :]
    y_ref[...] = (hs * jax.nn.gelu(gate_ref[...])).astype(BF)

    @pl.when(c == last_chunk)
    def _():
        hout_ref[...] = hs[last_row:last_row + 1, :]


def _lru_call(batch, length, n_real, proj, h0, buf, cw, cb, wa, ba, wx, bx, lam):
    t = CHUNK
    nc = length // t
    col = lambda j: pl.BlockSpec((t, 1024), lambda b, c: (b * nc + c, j))
    vec = pl.BlockSpec((None, 1, LRU_WIDTH), lambda b, c: (b, 0, 0))
    nb = LRU_WIDTH // LANES
    last = n_real - 1
    return pl.pallas_call(
        functools.partial(_lru_body, last_chunk=last // t, last_row=last % t),
        grid=(batch, nc),
        in_specs=[col(3), col(4), vec, pl.BlockSpec((None, 8, 1024), lambda b, c: (b, 0, 0)),
                  _full_spec((CONV_WIDTH, 1024)), _full_spec((1, 1024)),
                  _full_spec((nb, LANES, LANES)), _full_spec((1, 1024)),
                  _full_spec((nb, LANES, LANES)), _full_spec((1, 1024)), _full_spec((1, 1024))],
        out_specs=[pl.BlockSpec((t, 1024), lambda b, c: (b * nc + c, 0)), vec],
        out_shape=[jax.ShapeDtypeStruct((batch * length, 1024), BF),
                   jax.ShapeDtypeStruct((batch, 1, LRU_WIDTH), F32)],
        scratch_shapes=[pltpu.VMEM((t + 8, 1024), F32), pltpu.VMEM((1, LRU_WIDTH), F32)],
        compiler_params=_cparams("arbitrary", "arbitrary"),
        name="rg_lru",
    )(proj, proj, h0, buf, cw, cb, wa, ba, wx, bx, lam)


def _rope_tables(pos):
    half = HEAD_DIM // 2
    inv = ROPE_THETA ** (-jnp.arange(half, dtype=F32) / half)
    ang = pos.astype(F32)[:, None] * inv[None, :]
    cos = jnp.cos(ang)
    sin = jnp.sin(ang)
    cos = jnp.concatenate([cos, cos, cos, cos], axis=1)
    sin = jnp.concatenate([-sin, sin, -sin, sin], axis=1)
    return cos, sin


def _block_diag_pairs(w):
    nb = w.shape[0] // 2
    w2 = w.reshape(nb, 2, HEAD_DIM, HEAD_DIM)
    z = jnp.zeros((nb, HEAD_DIM, HEAD_DIM), w.dtype)
    top = jnp.concatenate([w2[:, 0], z], axis=2)
    bot = jnp.concatenate([z, w2[:, 1]], axis=2)
    return jnp.concatenate([top, bot], axis=1)


def _buf8(buf):
    return jnp.pad(buf, ((0, 0), (8 - (CONV_WIDTH - 1), 0), (0, 0)))


def _pad128(v):
    return jnp.pad(v, (0, LANES - v.shape[0])).reshape(1, LANES)


def kernel(x_prompt, x_sample, c_prompt, c_sample, cache_sb_k, cache_sb_v, cache_diff_k, cache_diff_v, state_ssd, state_ssd_conv, state_lru, state_lru_conv, page_table, w_ada, b_ada, norm_mix, norm_ffn, attn_w_in, attn_w_out, diff_lam, diff_subln, rec_w_in, ssd_conv_w, ssd_conv_b, ssd_dt_bias, ssd_a_log, ssd_d, ssd_norm, lru_conv_w, lru_conv_b, lru_w_a, lru_b_a, lru_w_x, lru_b_x, lru_lam, rec_w_out, ffn_w_gate, ffn_w_up, ffn_w_down, w_ada_final, b_ada_final, norm_final):
    bp, lp, d = x_prompt.shape
    bs, ls, _ = x_sample.shape
    n_pages = page_table.shape[1]
    n_phys = cache_sb_k.shape[1]
    na = attn_w_in.shape[0]
    nr = rec_w_in.shape[0]
    gp = _Group(bp, lp, per_row=False)
    gs = _Group(bs, ls, per_row=True)

    nb_all = bp + bs
    mp = -(-nb_all // 8) * 8
    c_all = jnp.pad(jnp.concatenate([c_prompt, c_sample], axis=0), ((0, mp - nb_all), (0, 0)))
    mods = _ada_call(c_all, w_ada, b_ada.reshape(DEPTH, 1, 6 * d))
    modf = _ada_call(c_all, w_ada_final[None], b_ada_final.reshape(1, 1, 2 * d))[0]

    def group_mods(m2d, n):
        parts = jnp.split(m2d, n, axis=-1)
        p_parts = [v[:bp].reshape(bp, 1, d) for v in parts]
        s_parts = [jnp.repeat(v[bp:bp + bs], ls, axis=0) for v in parts]
        return p_parts, s_parts

    w_attn_in = attn_w_in.astype(BF)
    w_attn_out = attn_w_out.astype(BF)
    w_rec_out = rec_w_out.astype(BF)
    wg = ffn_w_gate.astype(BF)
    wu = ffn_w_up.astype(BF)
    wd = ffn_w_down.astype(BF)
    s0, s1, s2 = 1024, 3072, 3088
    w_rec_main = jnp.concatenate([rec_w_in[:, :, :s1], rec_w_in[:, :, s2:]], axis=2).astype(BF)
    w_rec_dt = jnp.pad(rec_w_in[:, :, s1:s2], ((0, 0), (0, 0), (0, LANES - SSD_HEADS))).astype(BF)
    wa_bd = jax.vmap(_block_diag_pairs)(lru_w_a).astype(BF)
    wx_bd = jax.vmap(_block_diag_pairs)(lru_w_x).astype(BF)

    cos_p, sin_p = _rope_tables(jnp.arange(lp))
    pos_s = n_pages * PAGE + jnp.arange(ls)
    cos_s, sin_s = _rope_tables(jnp.tile(pos_s, bs))
    pt_flat = page_table.reshape(-1).astype(jnp.int32)

    kt_shape = (na * n_phys, N_CLS, HEAD_DIM, PAGE)
    kc_sb = cache_sb_k.transpose(0, 1, 3, 4, 2).reshape(kt_shape)
    vc_sb = cache_sb_v.transpose(0, 1, 3, 4, 2).reshape(kt_shape)
    kc_d = cache_diff_k.transpose(0, 1, 3, 4, 5, 2).reshape(kt_shape)
    vc_d = cache_diff_v.reshape(na * n_phys, PAGE * DIFF_HEADS, LANES)

    def q_rows(q):
        q4 = q.reshape(bs, ls, N_CLS, HEAD_DIM).transpose(0, 2, 1, 3)
        q4 = jnp.pad(q4, ((0, 0), (0, 0), (0, N_QPAD - ls), (0, 0)))
        eye = jnp.eye(N_CLS, dtype=q.dtype)
        qbd = q4[:, :, :, None, :] * eye[None, :, None, :, None]
        return qbd.reshape(bs, Q_ROWS, CLS_WIDTH)

    def new_page_t(k):
        k4 = k.reshape(bs, ls, N_CLS, HEAD_DIM).transpose(0, 2, 3, 1)
        return jnp.pad(k4, ((0, 0), (0, 0), (0, 0), (0, PAGE - ls)))

    def from_rows(o, n_cls):
        o4 = o.reshape(bs, n_cls, N_QPAD, -1).transpose(0, 2, 1, 3)[:, :ls]
        return o4.reshape(bs * ls, SB_WIDTH).astype(BF)

    xp = x_prompt.reshape(bp * lp, d)
    xs = x_sample.reshape(bs * ls, d)
    out_p = {k: [] for k in ("sbk", "sbv", "dk", "dv", "ssd", "ssdc", "lru", "lruc")}
    out_s = {k: [] for k in ("sbk", "sbv", "dk", "dv", "ssd", "ssdc", "lru", "lruc")}
    lpad = CHUNK

    for li in range(DEPTH):
        (sh1p, sc1p, g1p, sh2p, sc2p, g2p), (sh1s, sc1s, g1s, sh2s, sc2s, g2s) = group_mods(mods[li], 6)
        gmix = norm_mix[li].reshape(1, d)
        if li % 2 == 0:
            ai = li // 2
            lam_init = 0.8 - 0.6 * math.exp(-0.3 * li)
            gain = diff_subln[ai].reshape(1, LANES)
            qsb, ksb, ksbh, vsb, vsbh, qd, kd, kdh, vd, vdh = _attn_in_call(
                gp, xp, gmix, sh1p, sc1p, w_attn_in[ai], cos_p, sin_p)
            o_sb = _sb_prompt_call(bp, lp, qsb, ksbh, vsbh)
            o_d = _diff_prompt_call(bp, lp, qd, kdh, vdh, diff_lam[ai], gain, lam_init)
            xp = _out_proj_call(gp, o_sb, o_d, w_attn_out[ai], xp, g1p)
            out_p["sbk"].append(ksb.reshape(bp, lp, SB_HEADS, HEAD_DIM))
            out_p["sbv"].append(vsb.reshape(bp, lp, SB_HEADS, HEAD_DIM))
            out_p["dk"].append(kd.reshape(bp, lp, DIFF_HEADS, 2, HEAD_DIM))
            out_p["dv"].append(vd.reshape(bp, lp, DIFF_HEADS, 2 * HEAD_DIM))
            qsb, ksb, _, vsb, _, qd, kd, _, vd, _ = _attn_in_call(
                gs, xs, gmix, sh1s, sc1s, w_attn_in[ai], cos_s, sin_s)

            base = ai * n_phys
            o = _sb_sample_call(pt_flat, n_pages, base, q_rows(qsb), kc_sb, vc_sb,
                                new_page_t(ksb), new_page_t(vsb))
            o_sb = from_rows(o, N_CLS)
            vd_page = jnp.pad(vd.reshape(bs, ls * DIFF_HEADS, LANES),
                              ((0, 0), (0, (PAGE - ls) * DIFF_HEADS), (0, 0)))
            o = _diff_sample_call(pt_flat, n_pages, base, q_rows(qd), kc_d, vc_d,
                                  new_page_t(kd), vd_page, diff_lam[ai], gain, lam_init)
            o_d = from_rows(o, DIFF_HEADS)
            xs = _out_proj_call(gs, o_sb, o_d, w_attn_out[ai], xs, g1s)
            out_s["sbk"].append(ksb.reshape(bs, ls, SB_HEADS, HEAD_DIM))
            out_s["sbv"].append(vsb.reshape(bs, ls, SB_HEADS, HEAD_DIM))
            out_s["dk"].append(kd.reshape(bs, ls, DIFF_HEADS, 2, HEAD_DIM))
            out_s["dv"].append(vd.reshape(bs, ls, DIFF_HEADS, 2 * HEAD_DIM))
        else:
            ri = li // 2
            cw = ssd_conv_w[ri]
            cb = ssd_conv_b[ri].reshape(1, -1)
            ssd_args = (cw[:, :1024], cb[:, :1024], cw[:, 1024:], cb[:, 1024:],
                        _pad128(ssd_dt_bias[ri]), _pad128(ssd_a_log[ri]),
                        jnp.repeat(ssd_d[ri], HEAD_DIM).reshape(1, SSD_WIDTH), ssd_norm[ri].reshape(1, SSD_WIDTH))
            lru_args = (lru_conv_w[ri], lru_conv_b[ri].reshape(1, -1), wa_bd[ri], lru_b_a[ri].reshape(1, -1),
                        wx_bd[ri], lru_b_x[ri].reshape(1, -1), lru_lam[ri].reshape(1, -1))
            npairs = SSD_HEADS // 2
            proj, dt = _rec_in_call(gp, xp, gmix, sh1p, sc1p, w_rec_main[ri], w_rec_dt[ri])
            zeros8 = jnp.zeros((bp, 8, 1024), F32)
            y_ssd, h_ssd = _ssd_call(bp, lp, proj, dt, jnp.zeros((bp, npairs, LANES, SSD_STATE), F32),
                                     zeros8, zeros8, *ssd_args)
            y_lru, h_lru = _lru_call(bp, lp, lp, proj, jnp.zeros((bp, 1, LRU_WIDTH), F32), zeros8, *lru_args)
            xp = _out_proj_call(gp, y_ssd, y_lru, w_rec_out[ri], xp, g1p)
            p3 = proj.reshape(bp, lp, REC_MAIN)
            out_p["ssd"].append(h_ssd.reshape(bp, SSD_HEADS, HEAD_DIM, SSD_STATE))
            out_p["ssdc"].append(p3[:, lp - (CONV_WIDTH - 1):, s0:s1])
            out_p["lru"].append(h_lru.reshape(bp, LRU_WIDTH))
            out_p["lruc"].append(p3[:, lp - (CONV_WIDTH - 1):, 4096:])
            proj, dt = _rec_in_call(gs, xs, gmix, sh1s, sc1s, w_rec_main[ri], w_rec_dt[ri])
            p3 = proj.reshape(bs, ls, REC_MAIN)
            proj_pad = jnp.pad(p3, ((0, 0), (0, lpad - ls), (0, 0))).reshape(bs * lpad, REC_MAIN)
            dt_pad = jnp.pad(dt.reshape(bs, ls, LANES), ((0, 0), (0, lpad - ls), (0, 0)),
                             constant_values=-jnp.inf).reshape(bs * lpad, LANES)
            sbuf = _buf8(state_ssd_conv[ri])
            y_ssd, h_ssd = _ssd_call(bs, lpad, proj_pad, dt_pad,
                                     state_ssd[ri].reshape(bs, npairs, LANES, SSD_STATE),
                                     sbuf[:, :, :1024], sbuf[:, :, 1024:], *ssd_args)
            y_lru, h_lru = _lru_call(bs, lpad, ls, proj_pad, state_lru[ri].reshape(bs, 1, LRU_WIDTH),
                                     _buf8(state_lru_conv[ri]), *lru_args)
            y_ssd = y_ssd.reshape(bs, lpad, SSD_WIDTH)[:, :ls].reshape(bs * ls, SSD_WIDTH)
            y_lru = y_lru.reshape(bs, lpad, LRU_WIDTH)[:, :ls].reshape(bs * ls, LRU_WIDTH)
            xs = _out_proj_call(gs, y_ssd, y_lru, w_rec_out[ri], xs, g1s)
            out_s["ssd"].append(h_ssd.reshape(bs, SSD_HEADS, HEAD_DIM, SSD_STATE))
            out_s["ssdc"].append(p3[:, ls - (CONV_WIDTH - 1):, s0:s1])
            out_s["lru"].append(h_lru.reshape(bs, LRU_WIDTH))
            out_s["lruc"].append(p3[:, ls - (CONV_WIDTH - 1):, 4096:])
        gffn = norm_ffn[li].reshape(1, d)
        xp = _ffn_call(gp, xp, gffn, sh2p, sc2p, g2p, wg[li], wu[li], wd[li])
        xs = _ffn_call(gs, xs, gffn, sh2s, sc2s, g2s, wg[li], wu[li], wd[li])

    (shp, scp), (shs, scs) = group_mods(modf, 2)
    gfin = norm_final.reshape(1, d)
    y_prompt = _final_call(gp, xp, gfin, shp, scp).reshape(bp, lp, d)
    y_sample = _final_call(gs, xs, gfin, shs, scs).reshape(bs, ls, d)

    order = ("sbk", "sbv", "dk", "dv", "ssd", "ssdc", "lru", "lruc")
    return (y_prompt, y_sample) + tuple(jnp.stack(out_p[k]) for k in order) + tuple(jnp.stack(out_s[k]) for k in order)
```

```python
import functools
import math

import jax
import jax.numpy as jnp
from jax import lax
from jax.experimental import pallas as pl
from jax.experimental.pallas import tpu as pltpu

BF = jnp.bfloat16
F32 = jnp.float32

D_MODEL = 1024
DEPTH = 4
PAGE = 128
HEAD_DIM = 64
SB_HEADS = 8
SB_WIDTH = SB_HEADS * HEAD_DIM
DIFF_HEADS = 4
ATTN_IN = 6 * SB_WIDTH
ROPE_THETA = 10000.0
NEG_INF = -1e30
SSD_HEADS = 16
SSD_WIDTH = 1024
SSD_GROUPS = 4
SSD_STATE = 128
CONV_WIDTH = 4
LRU_WIDTH = 1024
LRU_BLOCKS = 16
LRU_C = 8.0
FFN_HIDDEN = 2816
EPS = 1e-6

LANES = 128
ROW_TILE = 512
ATT_TQ = 512
ATT_TK = 512
CHUNK = 128
FFN_TILE = 1408
VMEM_LIMIT = 52 * 1024 * 1024


def _cparams(*sem):
    return pltpu.CompilerParams(dimension_semantics=sem, vmem_limit_bytes=VMEM_LIMIT)


def _dot(a, b):
    return jnp.dot(a, b, preferred_element_type=F32)


def _dot_nt(a, b):
    return lax.dot_general(a, b, (((1,), (1,)), ((), ())), preferred_element_type=F32)


def _softplus(x):
    return jnp.maximum(x, 0.0) + jnp.log1p(jnp.exp(-jnp.abs(x)))


def _log_sigmoid(z):
    return jnp.minimum(z, 0.0) - jnp.log(1.0 + jnp.exp(-jnp.abs(z)))


def _modulate(x, g, sh, sc):
    y = x * lax.rsqrt(jnp.mean(x * x, axis=-1, keepdims=True) + EPS) * g
    return y * (1.0 + sc) + sh


def _split3(x):
    a = x.astype(BF)
    r = x - a.astype(F32)
    b = r.astype(BF)
    c = (r - b.astype(F32)).astype(BF)
    return a, b, c


def _split2(x):
    a = x.astype(BF)
    b = (x - a.astype(F32)).astype(BF)
    return a, b


def _ada_body(c_ref, w_ref, b_ref, o_ref):
    h = jax.nn.silu(c_ref[...]).astype(BF)
    o_ref[...] = _dot(h, w_ref[...].astype(BF)) + b_ref[...]


def _ada_call(c_all, w, b):
    nl, d, n = w.shape
    mp = c_all.shape[0]
    tn = 1024
    return pl.pallas_call(
        _ada_body,
        grid=(nl, n // tn),
        in_specs=[pl.BlockSpec((mp, d), lambda l, j: (0, 0)),
                  pl.BlockSpec((None, d, tn), lambda l, j: (l, 0, j)),
                  pl.BlockSpec((None, 1, tn), lambda l, j: (l, 0, j))],
        out_specs=pl.BlockSpec((None, mp, tn), lambda l, j: (l, 0, j)),
        out_shape=jax.ShapeDtypeStruct((nl, mp, n), F32),
        compiler_params=_cparams("arbitrary", "arbitrary"),
        name="ada",
    )(c_all, w, b)


class _Group:
    def __init__(self, batch, length, per_row):
        self.batch = batch
        self.length = length
        self.rows = batch * length
        self.per_row = per_row
        self.tm = min(ROW_TILE, self.rows)
        self.blocks_per_seq = 1 if per_row else length // self.tm

    def mod_spec(self, d=D_MODEL):
        if self.per_row:
            return pl.BlockSpec((self.tm, d), lambda i, *_: (i, 0))
        bps = self.blocks_per_seq
        return pl.BlockSpec((None, 1, d), lambda i, *_: (i // bps, 0, 0))

    def pos_spec(self):
        if self.per_row:
            return pl.BlockSpec((self.tm, LANES), lambda i, *_: (i, 0))
        bps = self.blocks_per_seq
        return pl.BlockSpec((self.tm, LANES), lambda i, *_: (i % bps, 0))


def _row_spec(tm, n):
    return pl.BlockSpec((tm, n), lambda i, *_: (i, 0))


def _full_spec(shape):
    nd = len(shape)
    return pl.BlockSpec(shape, lambda *_: (0,) * nd)


def _rope(x, cos, sin):
    n = x.shape[1] // LANES
    c = jnp.concatenate([cos] * n, axis=1)
    s = jnp.concatenate([sin] * n, axis=1)
    lane = lax.broadcasted_iota(jnp.int32, x.shape, 1)
    first = (lane % HEAD_DIM) < (HEAD_DIM // 2)
    w = x.shape[1]
    partner = jnp.where(first, pltpu.roll(x, w - HEAD_DIM // 2, 1), pltpu.roll(x, HEAD_DIM // 2, 1))
    return x * c + partner * s


def _attn_in_body(x_ref, g_ref, sh_ref, sc_ref, w_ref, cos_ref, sin_ref,
                  qsb_ref, ksb_ref, ksbh_ref, vsb_ref, vsbh_ref, qd_ref, kd_ref, kdh_ref, vd_ref, vdh_ref):
    h = _modulate(x_ref[...], g_ref[...], sh_ref[...], sc_ref[...]).astype(BF)
    proj = _dot(h, w_ref[...])
    w = SB_WIDTH
    scale = HEAD_DIM ** -0.5
    qsb_ref[...] = (proj[:, 0:w] * scale).astype(BF)
    k = proj[:, w:2 * w]
    ksb_ref[...] = k
    ksbh_ref[...] = k.astype(BF)
    v = proj[:, 2 * w:3 * w]
    vsb_ref[...] = v
    vsbh_ref[...] = v.astype(BF)
    cos = cos_ref[...]
    sin = sin_ref[...]
    qd_ref[...] = (_rope(proj[:, 3 * w:4 * w], cos, sin) * scale).astype(BF)
    kd = _rope(proj[:, 4 * w:5 * w], cos, sin)
    kd_ref[...] = kd
    kdh_ref[...] = kd.astype(BF)
    vd = proj[:, 5 * w:6 * w]
    vd_ref[...] = vd
    vdh_ref[...] = vd.astype(BF)


def _attn_in_call(grp, x, g, sh, sc, w, cos, sin):
    m, tm = grp.rows, grp.tm
    o32 = jax.ShapeDtypeStruct((m, SB_WIDTH), F32)
    o16 = jax.ShapeDtypeStruct((m, SB_WIDTH), BF)
    spec = _row_spec(tm, SB_WIDTH)
    return pl.pallas_call(
        _attn_in_body,
        grid=(m // tm,),
        in_specs=[_row_spec(tm, D_MODEL), _full_spec((1, D_MODEL)), grp.mod_spec(), grp.mod_spec(),
                  _full_spec((D_MODEL, ATTN_IN)), grp.pos_spec(), grp.pos_spec()],
        out_specs=[spec] * 10,
        out_shape=[o16, o32, o16, o32, o16, o16, o32, o16, o32, o16],
        compiler_params=_cparams("arbitrary"),
        name="attn_in",
    )(x, g, sh, sc, w, cos, sin)


def _sb_prompt_body(q_ref, k_ref, v_ref, o_ref):
    tq, tk = ATT_TQ, ATT_TK
    qi = pl.program_id(2)
    q = q_ref[...]
    lane = lax.broadcasted_iota(jnp.int32, (tq, LANES), 1)
    zero = jnp.zeros_like(q)
    qs = (jnp.where(lane < HEAD_DIM, q, zero), jnp.where(lane >= HEAD_DIM, q, zero))
    krow = lax.broadcasted_iota(jnp.int32, (tk, tk), 0)
    kcol = lax.broadcasted_iota(jnp.int32, (tk, tk), 1)
    later = (krow > kcol).astype(BF)
    row = lax.broadcasted_iota(jnp.int32, (tq, tk), 0) + qi * tq
    col = lax.broadcasted_iota(jnp.int32, (tq, tk), 1)

    def tile(kb, carry, masked):
        start = pl.multiple_of(kb * tk, tk)
        k = k_ref[pl.ds(start, tk), :]
        v = v_ref[pl.ds(start, tk), :]
        out = []
        for qh, (run, acc) in zip(qs, carry):
            z = _dot_nt(qh, k)
            lb = _log_sigmoid(z)
            lk = lb - z
            if masked:
                valid = col + kb * tk < row
                lk = jnp.where(valid, lk, 0.0)
            hi, lo = _split2(lk)
            rest = _dot(hi, later) + _dot(lo, later) + run
            wgt = jnp.exp(lb + rest)
            if masked:
                wgt = jnp.where(valid, wgt, 0.0)
            acc = acc + _dot(wgt.astype(BF), v)
            run = run + jnp.sum(lk, axis=1, keepdims=True)
            out.append((run, acc))
        return tuple(out)

    init = tuple((jnp.zeros((tq, 1), F32), jnp.zeros((tq, LANES), F32)) for _ in range(2))
    kd = (qi * tq) // tk
    carry = tile(kd, init, True)
    carry = lax.fori_loop(0, kd, lambda i, c: tile(kd - 1 - i, c, False), carry)
    o_ref[...] = jnp.where(lane < HEAD_DIM, carry[0][1], carry[1][1]).astype(BF)


def _prompt_attn_specs(batch, length):
    tq = ATT_TQ
    assert ATT_TK % tq == 0 and length % ATT_TK == 0
    nq = length // tq
    q_spec = pl.BlockSpec((tq, LANES), lambda b, p, i: (b * nq + i, p))
    kv_spec = pl.BlockSpec((length, LANES), lambda b, p, i: (b, p))
    return nq, q_spec, kv_spec


def _sb_prompt_call(batch, length, q, k, v):
    nq, q_spec, kv_spec = _prompt_attn_specs(batch, length)
    return pl.pallas_call(
        _sb_prompt_body,
        grid=(batch, SB_WIDTH // LANES, nq),
        in_specs=[q_spec, kv_spec, kv_spec],
        out_specs=q_spec,
        out_shape=jax.ShapeDtypeStruct(q.shape, BF),
        compiler_params=_cparams("arbitrary", "arbitrary", "arbitrary"),
        name="sb_prompt",
    )(q, k, v)


def _diff_lambda(lam_ref, lam_init):
    lv = lam_ref[...]
    s1 = jnp.sum(lv[0:1] * lv[1:2], axis=1, keepdims=True)
    s2 = jnp.sum(lv[2:3] * lv[3:4], axis=1, keepdims=True)
    return jnp.exp(s1) - jnp.exp(s2) + lam_init


def _diff_finish(o0, o1, lam, gain, lam_init):
    o = o0 - lam * o1
    y = o * lax.rsqrt(jnp.mean(o * o, axis=-1, keepdims=True) + EPS) * gain
    return y * (1.0 - lam_init)


def _diff_prompt_body(q_ref, k_ref, v_ref, lam_ref, gain_ref, o_ref, *, lam_init):
    tq, tk = ATT_TQ, ATT_TK
    qi = pl.program_id(2)
    q = q_ref[...]
    lane = lax.broadcasted_iota(jnp.int32, (tq, LANES), 1)
    zero = jnp.zeros_like(q)
    qs = (jnp.where(lane < HEAD_DIM, q, zero), jnp.where(lane >= HEAD_DIM, q, zero))
    row = lax.broadcasted_iota(jnp.int32, (tq, tk), 0) + qi * tq
    col = lax.broadcasted_iota(jnp.int32, (tq, tk), 1)

    def tile(kb, carry, masked):
        start = pl.multiple_of(kb * tk, tk)
        k = k_ref[pl.ds(start, tk), :]
        v = v_ref[pl.ds(start, tk), :]
        out = []
        for qh, (mx, den, acc) in zip(qs, carry):
            z = _dot_nt(qh, k)
            if masked:
                z = jnp.where(col + kb * tk <= row, z, NEG_INF)
            mx_new = jnp.maximum(mx, jnp.max(z, axis=1, keepdims=True))
            alpha = jnp.exp(mx - mx_new)
            p = jnp.exp(z - mx_new)
            den = alpha * den + jnp.sum(p, axis=1, keepdims=True)
            acc = alpha * acc + _dot(p.astype(BF), v)
            out.append((mx_new, den, acc))
        return tuple(out)

    init = tuple((jnp.full((tq, 1), NEG_INF, F32), jnp.zeros((tq, 1), F32), jnp.zeros((tq, LANES), F32))
                 for _ in range(2))
    kd = (qi * tq) // tk
    carry = tile(kd, init, True)
    carry = lax.fori_loop(0, kd, lambda i, c: tile(kd - 1 - i, c, False), carry)
    lam = _diff_lambda(lam_ref, lam_init)
    o0 = carry[0][2] / carry[0][1]
    o1 = carry[1][2] / carry[1][1]
    o_ref[...] = _diff_finish(o0, o1, lam, gain_ref[...], lam_init).astype(BF)


def _diff_prompt_call(batch, length, q, k, v, lam, gain, lam_init):
    nq, q_spec, kv_spec = _prompt_attn_specs(batch, length)
    return pl.pallas_call(
        functools.partial(_diff_prompt_body, lam_init=lam_init),
        grid=(batch, DIFF_HEADS, nq),
        in_specs=[q_spec, kv_spec, kv_spec, _full_spec((4, HEAD_DIM)), _full_spec((1, LANES))],
        out_specs=q_spec,
        out_shape=jax.ShapeDtypeStruct(q.shape, BF),
        compiler_params=_cparams("arbitrary", "arbitrary", "arbitrary"),
        name="diff_prompt",
    )(q, k, v, lam, gain)


N_CLS = 8
N_QPAD = 8
PAGES_PER_STEP = 16
Q_ROWS = N_CLS * N_QPAD


def _page_specs(n_pages, base, block):
    def spec(slot):
        def idx(b, j, pt_ref):
            page = n_pages - jnp.maximum(j, 1) * PAGES_PER_STEP + slot
            return (base + pt_ref[b * n_pages + page],) + (0,) * (len(block) - 1)
        return pl.BlockSpec(block, idx)
    return [spec(s) for s in range(PAGES_PER_STEP)]


CLS_WIDTH = N_CLS * HEAD_DIM


def _class_scores(qbd, k_refs):
    return jnp.concatenate(
        [_dot(qbd, k_ref[...].reshape(CLS_WIDTH, PAGE).astype(BF)) for k_ref in k_refs], axis=0)


def _own_block(rows_per_block, lanes_per_block):
    r = lax.broadcasted_iota(jnp.int32, (Q_ROWS, CLS_WIDTH), 0) // rows_per_block
    c = lax.broadcasted_iota(jnp.int32, (Q_ROWS, CLS_WIDTH), 1) // lanes_per_block
    return r == c


def _fold_slabs(a):
    return sum(a[:, p * LANES:(p + 1) * LANES] for p in range(a.shape[1] // LANES))


def _sb_sample_body(pt_ref, q_ref, *refs, n_steps):
    np_ = PAGES_PER_STEP
    kc, vc = refs[0:np_], refs[np_:2 * np_]
    kn_ref, vn_ref, o_ref, run_s, acc_s = refs[2 * np_:]
    j = pl.program_id(1)
    q = q_ref[...]
    row = lax.broadcasted_iota(jnp.int32, (PAGE, PAGE), 0)
    col = lax.broadcasted_iota(jnp.int32, (PAGE, PAGE), 1)
    later = (row > col).astype(BF)
    tok = lax.broadcasted_iota(jnp.int32, (Q_ROWS, PAGE), 1)
    qpos = lax.broadcasted_iota(jnp.int32, (Q_ROWS, PAGE), 0) % N_QPAD
    own = _own_block(N_QPAD, HEAD_DIM)

    def pages(k_refs, v_refs, new):
        z = _class_scores(q, k_refs)
        lb = _log_sigmoid(z)
        lk = lb - z
        if new:
            valid = tok < qpos
            lk = jnp.where(valid, lk, 0.0)
        hi, lo = _split2(lk)
        wgt = jnp.exp(lb + _dot(hi, later) + _dot(lo, later))
        if new:
            wgt = jnp.where(valid, wgt, 0.0)
        w16 = wgt.astype(BF)
        tot = jnp.sum(lk, axis=1, keepdims=True)
        parts = []
        for i, v_ref in enumerate(v_refs):
            rows = slice(i * Q_ROWS, (i + 1) * Q_ROWS)
            o_all = _dot_nt(w16[rows], v_ref[...].reshape(CLS_WIDTH, PAGE).astype(BF))
            parts.append((jnp.where(own, o_all, 0.0), tot[rows]))
        return parts

    def fold(parts):
        run, acc = run_s[...], acc_s[...]
        for pv, tot in parts:
            acc = acc + jnp.exp(run) * pv
            run = run + tot
        run_s[...] = run
        acc_s[...] = acc

    @pl.when(j == 0)
    def _():
        run_s[...] = jnp.zeros_like(run_s)
        acc_s[...] = jnp.zeros_like(acc_s)
        fold(pages([kn_ref], [vn_ref], True))

    @pl.when(j > 0)
    def _():
        fold(pages(kc[::-1], vc[::-1], False))

    @pl.when(j == n_steps - 1)
    def _():
        s = _fold_slabs(acc_s[...])
        o_ref[...] = s[:, 0:HEAD_DIM] + s[:, HEAD_DIM:LANES]


def _sb_sample_call(pt_flat, n_pages, base, q, kc, vc, kn, vn):
    nb = q.shape[0]
    n_steps = n_pages // PAGES_PER_STEP + 1
    block = (None, N_CLS, HEAD_DIM, PAGE)
    per_b = lambda *s: pl.BlockSpec((None,) + s, lambda b, j, pt: (b,) + (0,) * len(s))
    k_specs = _page_specs(n_pages, base, block)
    return pl.pallas_call(
        functools.partial(_sb_sample_body, n_steps=n_steps),
        grid_spec=pltpu.PrefetchScalarGridSpec(
            num_scalar_prefetch=1,
            grid=(nb, n_steps),
            in_specs=[per_b(Q_ROWS, CLS_WIDTH)] + k_specs + k_specs
                     + [per_b(N_CLS, HEAD_DIM, PAGE), per_b(N_CLS, HEAD_DIM, PAGE)],
            out_specs=per_b(Q_ROWS, HEAD_DIM),
            scratch_shapes=[pltpu.VMEM((Q_ROWS, 1), F32), pltpu.VMEM((Q_ROWS, CLS_WIDTH), F32)]),
        out_shape=jax.ShapeDtypeStruct((nb, Q_ROWS, HEAD_DIM), F32),
        compiler_params=_cparams("arbitrary", "arbitrary"),
        name="sb_sample",
    )(pt_flat, q, *([kc] * PAGES_PER_STEP), *([vc] * PAGES_PER_STEP), kn, vn)


def _diff_sample_body(pt_ref, q_ref, *refs, n_steps, lam_init):
    np_ = PAGES_PER_STEP
    kc, vc = refs[0:np_], refs[np_:2 * np_]
    kn_ref, vn_ref, lam_ref, gain_ref, o_ref, mx_s, den_s, acc_s = refs[2 * np_:]
    j = pl.program_id(1)
    q = q_ref[...]
    tok = lax.broadcasted_iota(jnp.int32, (Q_ROWS, PAGE), 1)
    qpos = lax.broadcasted_iota(jnp.int32, (Q_ROWS, PAGE), 0) % N_QPAD
    rows_per_head = 2 * N_QPAD
    own = _own_block(rows_per_head, LANES)

    def pages(k_refs, v_refs, new):
        z = _class_scores(q, k_refs)
        if new:
            z = jnp.where(tok <= qpos, z, NEG_INF)
        mloc = jnp.max(z, axis=1, keepdims=True)
        p = jnp.exp(z - mloc)
        p16 = p.astype(BF)
        lloc = jnp.sum(p, axis=1, keepdims=True)
        parts = []
        for i, v_ref in enumerate(v_refs):
            rows = slice(i * Q_ROWS, (i + 1) * Q_ROWS)
            v_heads = jnp.concatenate(
                [v_ref[pl.ds(h, PAGE, stride=DIFF_HEADS), :] for h in range(DIFF_HEADS)], axis=1).astype(BF)
            o_all = _dot(p16[rows], v_heads)
            parts.append((mloc[rows], lloc[rows], jnp.where(own, o_all, 0.0)))
        return parts

    def fold(parts):
        mx, den, acc = mx_s[...], den_s[...], acc_s[...]
        for mloc, lloc, pv in parts:
            mx_new = jnp.maximum(mx, mloc)
            a = jnp.exp(mx - mx_new)
            b = jnp.exp(mloc - mx_new)
            den = a * den + b * lloc
            acc = a * acc + b * pv
            mx = mx_new
        mx_s[...] = mx
        den_s[...] = den
        acc_s[...] = acc

    @pl.when(j == 0)
    def _():
        mx_s[...] = jnp.full_like(mx_s, NEG_INF)
        den_s[...] = jnp.zeros_like(den_s)
        acc_s[...] = jnp.zeros_like(acc_s)
        fold(pages([kn_ref], [vn_ref], True))

    @pl.when(j > 0)
    def _():
        fold(pages(kc[::-1], vc[::-1], False))

    @pl.when(j == n_steps - 1)
    def _():
        o = _fold_slabs(acc_s[...]) / den_s[...]
        o0 = jnp.concatenate([o[h * rows_per_head:h * rows_per_head + N_QPAD] for h in range(DIFF_HEADS)], axis=0)
        o1 = jnp.concatenate([o[h * rows_per_head + N_QPAD:(h + 1) * rows_per_head] for h in range(DIFF_HEADS)],
                             axis=0)
        lam = _diff_lambda(lam_ref, lam_init)
        o_ref[...] = _diff_finish(o0, o1, lam, gain_ref[...], lam_init)


def _diff_sample_call(pt_flat, n_pages, base, q, kc, vc, kn, vn, lam, gain, lam_init):
    nb = q.shape[0]
    n_steps = n_pages // PAGES_PER_STEP + 1
    v_rows = PAGE * DIFF_HEADS
    per_b = lambda *s: pl.BlockSpec((None,) + s, lambda b, j, pt: (b,) + (0,) * len(s))
    const = lambda shape: pl.BlockSpec(shape, lambda b, j, pt: (0,) * len(shape))
    k_specs = _page_specs(n_pages, base, (None, N_CLS, HEAD_DIM, PAGE))
    v_specs = _page_specs(n_pages, base, (None, v_rows, LANES))
    return pl.pallas_call(
        functools.partial(_diff_sample_body, n_steps=n_steps, lam_init=lam_init),
        grid_spec=pltpu.PrefetchScalarGridSpec(
            num_scalar_prefetch=1,
            grid=(nb, n_steps),
            in_specs=[per_b(Q_ROWS, CLS_WIDTH)] + k_specs + v_specs
                     + [per_b(N_CLS, HEAD_DIM, PAGE), per_b(v_rows, LANES), const((4, HEAD_DIM)), const((1, LANES))],
            out_specs=per_b(Q_ROWS // 2, LANES),
            scratch_shapes=[pltpu.VMEM((Q_ROWS, 1), F32), pltpu.VMEM((Q_ROWS, 1), F32),
                            pltpu.VMEM((Q_ROWS, CLS_WIDTH), F32)]),
        out_shape=jax.ShapeDtypeStruct((nb, Q_ROWS // 2, LANES), F32),
        compiler_params=_cparams("arbitrary", "arbitrary"),
        name="diff_sample",
    )(pt_flat, q, *([kc] * PAGES_PER_STEP), *([vc] * PAGES_PER_STEP), kn, vn, lam, gain)


def _out_proj_body(a_ref, b_ref, w_ref, x_ref, g_ref, o_ref):
    ka = a_ref.shape[1]
    y = _dot(a_ref[...], w_ref[0:ka, :]) + _dot(b_ref[...], w_ref[ka:, :])
    o_ref[...] = x_ref[...] + g_ref[...] * y


def _out_proj_call(grp, a, b, w, x, g):
    m, tm = grp.rows, grp.tm
    return pl.pallas_call(
        _out_proj_body,
        grid=(m // tm,),
        in_specs=[_row_spec(tm, a.shape[1]), _row_spec(tm, b.shape[1]), _full_spec(w.shape),
                  _row_spec(tm, D_MODEL), grp.mod_spec()],
        out_specs=_row_spec(tm, D_MODEL),
        out_shape=jax.ShapeDtypeStruct((m, D_MODEL), F32),
        compiler_params=_cparams("arbitrary"),
        name="out_proj",
    )(a, b, w, x, g)


def _ffn_body(x_ref, gn_ref, sh_ref, sc_ref, g2_ref, wg_ref, wu_ref, wd_ref, o_ref, h_s, acc_s):
    j = pl.program_id(1)

    @pl.when(j == 0)
    def _():
        h_s[...] = _modulate(x_ref[...], gn_ref[...], sh_ref[...], sc_ref[...]).astype(BF)
        acc_s[...] = jnp.zeros_like(acc_s)

    h = h_s[...]
    a = jax.nn.silu(_dot(h, wg_ref[...])) * _dot(h, wu_ref[...])
    acc_s[...] += _dot(a.astype(BF), wd_ref[...])

    @pl.when(j == pl.num_programs(1) - 1)
    def _():
        o_ref[...] = x_ref[...] + g2_ref[...] * acc_s[...]


def _ffn_call(grp, x, gn, sh, sc, g2, wg, wu, wd):
    m, tm = grp.rows, grp.tm
    tf = FFN_TILE
    return pl.pallas_call(
        _ffn_body,
        grid=(m // tm, FFN_HIDDEN // tf),
        in_specs=[_row_spec(tm, D_MODEL), _full_spec((1, D_MODEL)), grp.mod_spec(), grp.mod_spec(), grp.mod_spec(),
                  pl.BlockSpec((D_MODEL, tf), lambda i, j: (0, j)),
                  pl.BlockSpec((D_MODEL, tf), lambda i, j: (0, j)),
                  pl.BlockSpec((tf, D_MODEL), lambda i, j: (j, 0))],
        out_specs=_row_spec(tm, D_MODEL),
        out_shape=jax.ShapeDtypeStruct((m, D_MODEL), F32),
        scratch_shapes=[pltpu.VMEM((tm, D_MODEL), BF), pltpu.VMEM((tm, D_MODEL), F32)],
        compiler_params=_cparams("arbitrary", "arbitrary"),
        name="ffn",
    )(x, gn, sh, sc, g2, wg, wu, wd)


def _final_body(x_ref, g_ref, sh_ref, sc_ref, o_ref):
    o_ref[...] = _modulate(x_ref[...], g_ref[...], sh_ref[...], sc_ref[...])


def _final_call(grp, x, g, sh, sc):
    m, tm = grp.rows, grp.tm
    return pl.pallas_call(
        _final_body,
        grid=(m // tm,),
        in_specs=[_row_spec(tm, D_MODEL), _full_spec((1, D_MODEL)), grp.mod_spec(), grp.mod_spec()],
        out_specs=_row_spec(tm, D_MODEL),
        out_shape=jax.ShapeDtypeStruct((m, D_MODEL), F32),
        compiler_params=_cparams("arbitrary"),
        name="final_norm",
    )(x, g, sh, sc)


REC_MAIN = 5 * 1024


def _rec_in_body(x_ref, g_ref, sh_ref, sc_ref, w_ref, wdt_ref, o_ref, dt_ref, h_s):
    j = pl.program_id(1)

    @pl.when(j == 0)
    def _():
        h = _modulate(x_ref[...], g_ref[...], sh_ref[...], sc_ref[...]).astype(BF)
        h_s[...] = h
        dt_ref[...] = _dot(h, wdt_ref[...])

    o_ref[...] = _dot(h_s[...], w_ref[...])


def _rec_in_call(grp, x, g, sh, sc, w_main, w_dt):
    m, tm = grp.rows, grp.tm
    tn = 1024
    return pl.pallas_call(
        _rec_in_body,
        grid=(m // tm, REC_MAIN // tn),
        in_specs=[_row_spec(tm, D_MODEL), _full_spec((1, D_MODEL)), grp.mod_spec(), grp.mod_spec(),
                  pl.BlockSpec((D_MODEL, tn), lambda i, j: (0, j)), _full_spec((D_MODEL, LANES))],
        out_specs=[pl.BlockSpec((tm, tn), lambda i, j: (i, j)), _row_spec(tm, LANES)],
        out_shape=[jax.ShapeDtypeStruct((m, REC_MAIN), F32), jax.ShapeDtypeStruct((m, LANES), F32)],
        scratch_shapes=[pltpu.VMEM((tm, D_MODEL), BF)],
        compiler_params=_cparams("arbitrary", "arbitrary"),
        name="rec_in",
    )(x, g, sh, sc, w_main, w_dt)


def _causal_conv(ext_ref, x, buf_ref, w_ref, b_ref, first):
    t = x.shape[0]

    @pl.when(first)
    def _():
        ext_ref[0:8, :] = buf_ref[...]

    ext_ref[8:8 + t, :] = x
    y = b_ref[...] + x * w_ref[CONV_WIDTH - 1:CONV_WIDTH, :]
    for back in range(1, CONV_WIDTH):
        k = CONV_WIDTH - 1 - back
        y = y + ext_ref[pl.ds(8 - back, t), :] * w_ref[k:k + 1, :]
    ext_ref[0:8, :] = ext_ref[t:t + 8, :]
    return y


def _pair_expand(v, npairs):
    rows = v.shape[0]
    lane = lax.broadcasted_iota(jnp.int32, (rows, LANES), 1)
    parts = [jnp.where(lane < HEAD_DIM, v[:, 2 * p:2 * p + 1], v[:, 2 * p + 1:2 * p + 2]) for p in range(npairs)]
    return jnp.concatenate(parts, axis=1)


def _ssd_body(z_ref, xs_ref, bc_ref, dt_ref, h0_ref, bufx_ref, bufbc_ref, cwx_ref, cbx_ref, cwbc_ref, cbbc_ref,
              dtb_ref, alog_ref, dexp_ref, nw_ref, y_ref, hout_ref, extx, extbc, hst):
    q = CHUNK
    c = pl.program_id(1)
    first = c == 0

    @pl.when(first)
    def _():
        hst[...] = h0_ref[...]

    xs = jax.nn.silu(_causal_conv(extx, xs_ref[...], bufx_ref, cwx_ref, cbx_ref, first))
    bc = jax.nn.silu(_causal_conv(extbc, bc_ref[...], bufbc_ref, cwbc_ref, cbbc_ref, first))
    gw = SSD_GROUPS * SSD_STATE
    bm = bc[:, 0:gw].astype(BF)
    cm = bc[:, gw:2 * gw].astype(BF)

    dt = _softplus(dt_ref[...] + dtb_ref[...])
    a = dt * (-jnp.exp(alog_ref[...]))
    row = lax.broadcasted_iota(jnp.int32, (q, q), 0)
    col = lax.broadcasted_iota(jnp.int32, (q, q), 1)
    causal = col <= row
    tri = causal.astype(BF)
    a1, a2, a3 = _split3(a)
    cs = _dot(tri, a1) + _dot(tri, a2) + _dot(tri, a3)
    cs_t = cs.T
    cs_last = cs[q - 1:q, :]
    dend = jnp.exp(cs_last - cs)
    ecs = jnp.exp(cs)
    chunk_decay_t = jnp.exp(cs_t[:, q - 1:q])

    npairs = SSD_HEADS // 2
    xdt = xs * _pair_expand(dt, npairs)
    xdt16 = xdt.astype(BF)
    xd_t = (xdt * _pair_expand(dend, npairs)).T.astype(BF)
    ecs_x = _pair_expand(ecs, npairs)
    lane = lax.broadcasted_iota(jnp.int32, (q, LANES), 1)
    srow = lax.broadcasted_iota(jnp.int32, (LANES, SSD_STATE), 0)
    heads_per_group = SSD_HEADS // SSD_GROUPS
    ys = []
    for p in range(npairs):
        g = (2 * p) // heads_per_group
        bg = bm[:, g * SSD_STATE:(g + 1) * SSD_STATE]
        cg = cm[:, g * SSD_STATE:(g + 1) * SSD_STATE]
        cb = _dot_nt(cg, bg)
        xp = xdt16[:, p * LANES:(p + 1) * LANES]
        zero = jnp.zeros_like(xp)
        y = None
        for hh in range(2):
            h = 2 * p + hh
            seg = cs[:, h:h + 1] - cs_t[h:h + 1, :]
            decay = jnp.where(causal, jnp.exp(jnp.where(causal, seg, 0.0)), 0.0)
            xh = jnp.where((lane < HEAD_DIM) == (hh == 0), xp, zero)
            part = _dot((cb * decay).astype(BF), xh)
            y = part if y is None else y + part
        hp = hst[p]
        y = y + _dot_nt(cg, hp.astype(BF)) * ecs_x[:, p * LANES:(p + 1) * LANES]
        states = _dot(xd_t[p * LANES:(p + 1) * LANES, :], bg)
        cdec = jnp.where(srow < HEAD_DIM, chunk_decay_t[2 * p:2 * p + 1, :], chunk_decay_t[2 * p + 1:2 * p + 2, :])
        hst[p] = hp * cdec + states
        ys.append(y)
    y = jnp.concatenate(ys, axis=1)
    y = y + dexp_ref[...] * xs
    y = y * jax.nn.silu(z_ref[...])
    gsz = SSD_WIDTH // SSD_GROUPS
    outs = []
    for g in range(SSD_GROUPS):
        yg = y[:, g * gsz:(g + 1) * gsz]
        outs.append(yg * lax.rsqrt(jnp.mean(yg * yg, axis=-1, keepdims=True) + EPS))
    y_ref[...] = (jnp.concatenate(outs, axis=1) * nw_ref[...]).astype(BF)

    @pl.when(c == pl.num_programs(1) - 1)
    def _():
        hout_ref[...] = hst[...]


def _ssd_call(batch, length, proj, dt, h0, bufx, bufbc, cwx, cbx, cwbc, cbbc, dtb, alog, dexp, nw):
    q = CHUNK
    nc = length // q
    col = lambda j: pl.BlockSpec((q, 1024), lambda b, c: (b * nc + c, j))
    npairs = SSD_HEADS // 2
    st_spec = pl.BlockSpec((None, npairs, LANES, SSD_STATE), lambda b, c: (b, 0, 0, 0))
    buf_spec = pl.BlockSpec((None, 8, 1024), lambda b, c: (b, 0, 0))
    return pl.pallas_call(
        _ssd_body,
        grid=(batch, nc),
        in_specs=[col(0), col(1), col(2), pl.BlockSpec((q, LANES), lambda b, c: (b * nc + c, 0)),
                  st_spec, buf_spec, buf_spec,
                  _full_spec((CONV_WIDTH, 1024)), _full_spec((1, 1024)), _full_spec((CONV_WIDTH, 1024)),
                  _full_spec((1, 1024)), _full_spec((1, LANES)), _full_spec((1, LANES)),
                  _full_spec((1, 1024)), _full_spec((1, 1024))],
        out_specs=[pl.BlockSpec((q, 1024), lambda b, c: (b * nc + c, 0)), st_spec],
        out_shape=[jax.ShapeDtypeStruct((batch * length, 1024), BF),
                   jax.ShapeDtypeStruct((batch, npairs, LANES, SSD_STATE), F32)],
        scratch_shapes=[pltpu.VMEM((q + 8, 1024), F32), pltpu.VMEM((q + 8, 1024), F32),
                        pltpu.VMEM((npairs, LANES, SSD_STATE), F32)],
        compiler_params=_cparams("arbitrary", "arbitrary"),
        name="ssd",
    )(proj, proj, proj, dt, h0, bufx, bufbc, cwx, cbx, cwbc, cbbc, dtb, alog, dexp, nw)


def _lru_body(gate_ref, xl_ref, h0_ref, buf_ref, cw_ref, cb_ref, wa_ref, ba_ref, wx_ref, bx_ref, lam_ref,
              y_ref, hout_ref, ext, hprev, *, last_chunk, last_row):
    t = CHUNK
    c = pl.program_id(1)
    first = c == 0

    @pl.when(first)
    def _():
        hprev[...] = h0_ref[...]

    x = _causal_conv(ext, xl_ref[...], buf_ref, cw_ref, cb_ref, first)
    x16 = x.astype(BF)
    nb = LRU_WIDTH // LANES
    ra = jnp.concatenate([_dot(x16[:, p * LANES:(p + 1) * LANES], wa_ref[p]) for p in range(nb)], axis=1)
    rx = jnp.concatenate([_dot(x16[:, p * LANES:(p + 1) * LANES], wx_ref[p]) for p in range(nb)], axis=1)
    r = jax.nn.sigmoid(ra + ba_ref[...])
    i = jax.nn.sigmoid(rx + bx_ref[...])
    log_a = -LRU_C * r * _softplus(-lam_ref[...])
    a = jnp.exp(log_a)
    u = jnp.sqrt(1.0 - jnp.exp(2.0 * log_a)) * (i * x)
    row = lax.broadcasted_iota(jnp.int32, (t, LRU_WIDTH), 0)
    sh = 1
    while sh < t:
        m = row >= sh
        u = jnp.where(m, a * pltpu.roll(u, sh, 0) + u, u)
        a = jnp.where(m, a * pltpu.roll(a, sh, 0), a)
        sh *= 2
    hs = a * hprev[...] + u
    hprev[...] = hs[t - 1:t, :]
    y_ref[...] = (hs * jax.nn.gelu(gate_ref[...])).astype(BF)

    @pl.when(c == last_chunk)
    def _():
        hout_ref[...] = hs[last_row:last_row + 1, :]


def _lru_call(batch, length, n_real, proj, h0, buf, cw, cb, wa, ba, wx, bx, lam):
    t = CHUNK
    nc = length // t
    col = lambda j: pl.BlockSpec((t, 1024), lambda b, c: (b * nc + c, j))
    vec = pl.BlockSpec((None, 1, LRU_WIDTH), lambda b, c: (b, 0, 0))
    nb = LRU_WIDTH // LANES
    last = n_real - 1
    return pl.pallas_call(
        functools.partial(_lru_body, last_chunk=last // t, last_row=last % t),
        grid=(batch, nc),
        in_specs=[col(3), col(4), vec, pl.BlockSpec((None, 8, 1024), lambda b, c: (b, 0, 0)),
                  _full_spec((CONV_WIDTH, 1024)), _full_spec((1, 1024)),
                  _full_spec((nb, LANES, LANES)), _full_spec((1, 1024)),
                  _full_spec((nb, LANES, LANES)), _full_spec((1, 1024)), _full_spec((1, 1024))],
        out_specs=[pl.BlockSpec((t, 1024), lambda b, c: (b * nc + c, 0)), vec],
        out_shape=[jax.ShapeDtypeStruct((batch * length, 1024), BF),
                   jax.ShapeDtypeStruct((batch, 1, LRU_WIDTH), F32)],
        scratch_shapes=[pltpu.VMEM((t + 8, 1024), F32), pltpu.VMEM((1, LRU_WIDTH), F32)],
        compiler_params=_cparams("arbitrary", "arbitrary"),
        name="rg_lru",
    )(proj, proj, h0, buf, cw, cb, wa, ba, wx, bx, lam)


def _rope_tables(pos):
    half = HEAD_DIM // 2
    inv = ROPE_THETA ** (-jnp.arange(half, dtype=F32) / half)
    ang = pos.astype(F32)[:, None] * inv[None, :]
    cos = jnp.cos(ang)
    sin = jnp.sin(ang)
    cos = jnp.concatenate([cos, cos, cos, cos], axis=1)
    sin = jnp.concatenate([-sin, sin, -sin, sin], axis=1)
    return cos, sin


def _block_diag_pairs(w):
    nb = w.shape[0] // 2
    w2 = w.reshape(nb, 2, HEAD_DIM, HEAD_DIM)
    z = jnp.zeros((nb, HEAD_DIM, HEAD_DIM), w.dtype)
    top = jnp.concatenate([w2[:, 0], z], axis=2)
    bot = jnp.concatenate([z, w2[:, 1]], axis=2)
    return jnp.concatenate([top, bot], axis=1)


def _buf8(buf):
    return jnp.pad(buf, ((0, 0), (8 - (CONV_WIDTH - 1), 0), (0, 0)))


def _pad128(v):
    return jnp.pad(v, (0, LANES - v.shape[0])).reshape(1, LANES)


def kernel(x_prompt, x_sample, c_prompt, c_sample, cache_sb_k, cache_sb_v, cache_diff_k, cache_diff_v, state_ssd, state_ssd_conv, state_lru, state_lru_conv, page_table, w_ada, b_ada, norm_mix, norm_ffn, attn_w_in, attn_w_out, diff_lam, diff_subln, rec_w_in, ssd_conv_w, ssd_conv_b, ssd_dt_bias, ssd_a_log, ssd_d, ssd_norm, lru_conv_w, lru_conv_b, lru_w_a, lru_b_a, lru_w_x, lru_b_x, lru_lam, rec_w_out, ffn_w_gate, ffn_w_up, ffn_w_down, w_ada_final, b_ada_final, norm_final):
    bp, lp, d = x_prompt.shape
    bs, ls, _ = x_sample.shape
    n_pages = page_table.shape[1]
    n_phys = cache_sb_k.shape[1]
    na = attn_w_in.shape[0]
    nr = rec_w_in.shape[0]
    gp = _Group(bp, lp, per_row=False)
    gs = _Group(bs, ls, per_row=True)

    nb_all = bp + bs
    mp = -(-nb_all // 8) * 8
    c_all = jnp.pad(jnp.concatenate([c_prompt, c_sample], axis=0), ((0, mp - nb_all), (0, 0)))
    mods = _ada_call(c_all, w_ada, b_ada.reshape(DEPTH, 1, 6 * d))
    modf = _ada_call(c_all, w_ada_final[None], b_ada_final.reshape(1, 1, 2 * d))[0]

    def group_mods(m2d, n):
        parts = jnp.split(m2d, n, axis=-1)
        p_parts = [v[:bp].reshape(bp, 1, d) for v in parts]
        s_parts = [jnp.repeat(v[bp:bp + bs], ls, axis=0) for v in parts]
        return p_parts, s_parts

    w_attn_in = attn_w_in.astype(BF)
    w_attn_out = attn_w_out.astype(BF)
    w_rec_out = rec_w_out.astype(BF)
    wg = ffn_w_gate.astype(BF)
    wu = ffn_w_up.astype(BF)
    wd = ffn_w_down.astype(BF)
    s0, s1, s2 = 1024, 3072, 3088
    w_rec_main = jnp.concatenate([rec_w_in[:, :, :s1], rec_w_in[:, :, s2:]], axis=2).astype(BF)
    w_rec_dt = jnp.pad(rec_w_in[:, :, s1:s2], ((0, 0), (0, 0), (0, LANES - SSD_HEADS))).astype(BF)
    wa_bd = jax.vmap(_block_diag_pairs)(lru_w_a).astype(BF)
    wx_bd = jax.vmap(_block_diag_pairs)(lru_w_x).astype(BF)

    cos_p, sin_p = _rope_tables(jnp.arange(lp))
    pos_s = n_pages * PAGE + jnp.arange(ls)
    cos_s, sin_s = _rope_tables(jnp.tile(pos_s, bs))
    pt_flat = page_table.reshape(-1).astype(jnp.int32)

    kt_shape = (na * n_phys, N_CLS, HEAD_DIM, PAGE)
    kc_sb = cache_sb_k.transpose(0, 1, 3, 4, 2).reshape(kt_shape)
    vc_sb = cache_sb_v.transpose(0, 1, 3, 4, 2).reshape(kt_shape)
    kc_d = cache_diff_k.transpose(0, 1, 3, 4, 5, 2).reshape(kt_shape)
    vc_d = cache_diff_v.reshape(na * n_phys, PAGE * DIFF_HEADS, LANES)

    def q_rows(q):
        q4 = q.reshape(bs, ls, N_CLS, HEAD_DIM).transpose(0, 2, 1, 3)
        q4 = jnp.pad(q4, ((0, 0), (0, 0), (0, N_QPAD - ls), (0, 0)))
        eye = jnp.eye(N_CLS, dtype=q.dtype)
        qbd = q4[:, :, :, None, :] * eye[None, :, None, :, None]
        return qbd.reshape(bs, Q_ROWS, CLS_WIDTH)

    def new_page_t(k):
        k4 = k.reshape(bs, ls, N_CLS, HEAD_DIM).transpose(0, 2, 3, 1)
        return jnp.pad(k4, ((0, 0), (0, 0), (0, 0), (0, PAGE - ls)))

    def from_rows(o, n_cls):
        o4 = o.reshape(bs, n_cls, N_QPAD, -1).transpose(0, 2, 1, 3)[:, :ls]
        return o4.reshape(bs * ls, SB_WIDTH).astype(BF)

    xp = x_prompt.reshape(bp * lp, d)
    xs = x_sample.reshape(bs * ls, d)
    out_p = {k: [] for k in ("sbk", "sbv", "dk", "dv", "ssd", "ssdc", "lru", "lruc")}
    out_s = {k: [] for k in ("sbk", "sbv", "dk", "dv", "ssd", "ssdc", "lru", "lruc")}
    lpad = CHUNK

    for li in range(DEPTH):
        (sh1p, sc1p, g1p, sh2p, sc2p, g2p), (sh1s, sc1s, g1s, sh2s, sc2s, g2s) = group_mods(mods[li], 6)
        gmix = norm_mix[li].reshape(1, d)
        if li % 2 == 0:
            ai = li // 2
            lam_init = 0.8 - 0.6 * math.exp(-0.3 * li)
            gain = diff_subln[ai].reshape(1, LANES)
            qsb, ksb, ksbh, vsb, vsbh, qd, kd, kdh, vd, vdh = _attn_in_call(
                gp, xp, gmix, sh1p, sc1p, w_attn_in[ai], cos_p, sin_p)
            o_sb = _sb_prompt_call(bp, lp, qsb, ksbh, vsbh)
            o_d = _diff_prompt_call(bp, lp, qd, kdh, vdh, diff_lam[ai], gain, lam_init)
            xp = _out_proj_call(gp, o_sb, o_d, w_attn_out[ai], xp, g1p)
            out_p["sbk"].append(ksb.reshape(bp, lp, SB_HEADS, HEAD_DIM))
            out_p["sbv"].append(vsb.reshape(bp, lp, SB_HEADS, HEAD_DIM))
            out_p["dk"].append(kd.reshape(bp, lp, DIFF_HEADS, 2, HEAD_DIM))
            out_p["dv"].append(vd.reshape(bp, lp, DIFF_HEADS, 2 * HEAD_DIM))
            qsb, ksb, _, vsb, _, qd, kd, _, vd, _ = _attn_in_call(
                gs, xs, gmix, sh1s, sc1s, w_attn_in[ai], cos_s, sin_s)

            base = ai * n_phys
            o = _sb_sample_call(pt_flat, n_pages, base, q_rows(qsb), kc_sb, vc_sb,
                                new_page_t(ksb), new_page_t(vsb))
            o_sb = from_rows(o, N_CLS)
            vd_page = jnp.pad(vd.reshape(bs, ls * DIFF_HEADS, LANES),
                              ((0, 0), (0, (PAGE - ls) * DIFF_HEADS), (0, 0)))
            o = _diff_sample_call(pt_flat, n_pages, base, q_rows(qd), kc_d, vc_d,
                                  new_page_t(kd), vd_page, diff_lam[ai], gain, lam_init)
            o_d = from_rows(o, DIFF_HEADS)
            xs = _out_proj_call(gs, o_sb, o_d, w_attn_out[ai], xs, g1s)
            out_s["sbk"].append(ksb.reshape(bs, ls, SB_HEADS, HEAD_DIM))
            out_s["sbv"].append(vsb.reshape(bs, ls, SB_HEADS, HEAD_DIM))
            out_s["dk"].append(kd.reshape(bs, ls, DIFF_HEADS, 2, HEAD_DIM))
            out_s["dv"].append(vd.reshape(bs, ls, DIFF_HEADS, 2 * HEAD_DIM))
        else:
            ri = li // 2
            cw = ssd_conv_w[ri]
            cb = ssd_conv_b[ri].reshape(1, -1)
            ssd_args = (cw[:, :1024], cb[:, :1024], cw[:, 1024:], cb[:, 1024:],
                        _pad128(ssd_dt_bias[ri]), _pad128(ssd_a_log[ri]),
                        jnp.repeat(ssd_d[ri], HEAD_DIM).reshape(1, SSD_WIDTH), ssd_norm[ri].reshape(1, SSD_WIDTH))
            lru_args = (lru_conv_w[ri], lru_conv_b[ri].reshape(1, -1), wa_bd[ri], lru_b_a[ri].reshape(1, -1),
                        wx_bd[ri], lru_b_x[ri].reshape(1, -1), lru_lam[ri].reshape(1, -1))
            npairs = SSD_HEADS // 2
            proj, dt = _rec_in_call(gp, xp, gmix, sh1p, sc1p, w_rec_main[ri], w_rec_dt[ri])
            zeros8 = jnp.zeros((bp, 8, 1024), F32)
            y_ssd, h_ssd = _ssd_call(bp, lp, proj, dt, jnp.zeros((bp, npairs, LANES, SSD_STATE), F32),
                                     zeros8, zeros8, *ssd_args)
            y_lru, h_lru = _lru_call(bp, lp, lp, proj, jnp.zeros((bp, 1, LRU_WIDTH), F32), zeros8, *lru_args)
            xp = _out_proj_call(gp, y_ssd, y_lru, w_rec_out[ri], xp, g1p)
            p3 = proj.reshape(bp, lp, REC_MAIN)
            out_p["ssd"].append(h_ssd.reshape(bp, SSD_HEADS, HEAD_DIM, SSD_STATE))
            out_p["ssdc"].append(p3[:, lp - (CONV_WIDTH - 1):, s0:s1])
            out_p["lru"].append(h_lru.reshape(bp, LRU_WIDTH))
            out_p["lruc"].append(p3[:, lp - (CONV_WIDTH - 1):, 4096:])
            proj, dt = _rec_in_call(gs, xs, gmix, sh1s, sc1s, w_rec_main[ri], w_rec_dt[ri])
            p3 = proj.reshape(bs, ls, REC_MAIN)
            proj_pad = jnp.pad(p3, ((0, 0), (0, lpad - ls), (0, 0))).reshape(bs * lpad, REC_MAIN)
            dt_pad = jnp.pad(dt.reshape(bs, ls, LANES), ((0, 0), (0, lpad - ls), (0, 0)),
                             constant_values=-jnp.inf).reshape(bs * lpad, LANES)
            sbuf = _buf8(state_ssd_conv[ri])
            y_ssd, h_ssd = _ssd_call(bs, lpad, proj_pad, dt_pad,
                                     state_ssd[ri].reshape(bs, npairs, LANES, SSD_STATE),
                                     sbuf[:, :, :1024], sbuf[:, :, 1024:], *ssd_args)
            y_lru, h_lru = _lru_call(bs, lpad, ls, proj_pad, state_lru[ri].reshape(bs, 1, LRU_WIDTH),
                                     _buf8(state_lru_conv[ri]), *lru_args)
            y_ssd = y_ssd.reshape(bs, lpad, SSD_WIDTH)[:, :ls].reshape(bs * ls, SSD_WIDTH)
            y_lru = y_lru.reshape(bs, lpad, LRU_WIDTH)[:, :ls].reshape(bs * ls, LRU_WIDTH)
            xs = _out_proj_call(gs, y_ssd, y_lru, w_rec_out[ri], xs, g1s)
            out_s["ssd"].append(h_ssd.reshape(bs, SSD_HEADS, HEAD_DIM, SSD_STATE))
            out_s["ssdc"].append(p3[:, ls - (CONV_WIDTH - 1):, s0:s1])
            out_s["lru"].append(h_lru.reshape(bs, LRU_WIDTH))
            out_s["lruc"].append(p3[:, ls - (CONV_WIDTH - 1):, 4096:])
        gffn = norm_ffn[li].reshape(1, d)
        xp = _ffn_call(gp, xp, gffn, sh2p, sc2p, g2p, wg[li], wu[li], wd[li])
        xs = _ffn_call(gs, xs, gffn, sh2s, sc2s, g2s, wg[li], wu[li], wd[li])

    (shp, scp), (shs, scs) = group_mods(modf, 2)
    gfin = norm_final.reshape(1, d)
    y_prompt = _final_call(gp, xp, gfin, shp, scp).reshape(bp, lp, d)
    y_sample = _final_call(gs, xs, gfin, shs, scs).reshape(bs, ls, d)

    order = ("sbk", "sbv", "dk", "dv", "ssd", "ssdc", "lru", "lruc")
    return (y_prompt, y_sample) + tuple(jnp.stack(out_p[k]) for k in order) + tuple(jnp.stack(out_s[k]) for k in order)
```
